```python
import math
import jax, jax.numpy as jnp
from jax import lax
import numpy as np

D_MODEL = 2048
BATCH = 4
SEQ = 4096
DEPTH = 2

MOBA_HEADS = 8
MOBA_HEAD_DIM = 128
MOBA_BLOCK = 256
MOBA_TOPK = 3
MOBA_Q_CHUNK = 32
MOBA_WIDTH = MOBA_HEADS * MOBA_HEAD_DIM
DIFF_HEADS = 8
DIFF_QK_DIM = 64
DIFF_V_DIM = 2 * DIFF_QK_DIM
DIFF_Q_BLOCK = 128
DIFF_WIDTH = DIFF_HEADS * DIFF_V_DIM
IN_WIDTHS = (MOBA_WIDTH, MOBA_WIDTH, MOBA_WIDTH, DIFF_WIDTH, DIFF_WIDTH, DIFF_WIDTH, D_MODEL, D_MODEL)
IN_COLS = sum(IN_WIDTHS)
IN_SPLITS = tuple(int(c) for c in np.cumsum(IN_WIDTHS)[:-1])
PEER_HEADS = 8
PEER_N_KEYS = 128
PEER_N_EXPERTS = PEER_N_KEYS * PEER_N_KEYS
PEER_TOPK = 16
PEER_QUERY_DIM = 256
PEER_HALF = PEER_QUERY_DIM // 2
PEER_TOKEN_CHUNK = 128
ROPE_THETA = 10000.0
LN_EPS = 1e-5
RMS_EPS = 1e-6
DEEPNORM_ALPHA = (2 * DEPTH) ** 0.25
DEEPNORM_BETA = (8 * DEPTH) ** -0.25

kernel_name = "hybrid_moba_diffattn_peer_deepnorm"


def layer_norm(x, gain, bias):
    xf = x.astype(jnp.float32)
    mu = jnp.mean(xf, axis=-1, keepdims=True)
    var = jnp.mean(jnp.square(xf - mu), axis=-1, keepdims=True)
    y = (xf - mu) * lax.rsqrt(var + LN_EPS) * gain.astype(jnp.float32) + bias.astype(jnp.float32)
    return y.astype(x.dtype)


def rope_tables(positions, dim):
    inv_freq = ROPE_THETA ** (-jnp.arange(0, dim, 2, dtype=jnp.float32) / dim)
    ang = positions.astype(jnp.float32)[..., None] * inv_freq
    return jnp.cos(ang)[:, None], jnp.sin(ang)[:, None]


def apply_rope(t, cos, sin):
    t1, t2 = jnp.split(t, 2, axis=-1)
    c = cos.astype(t.dtype)
    s = sin.astype(t.dtype)
    return jnp.concatenate([t1 * c - t2 * s, t1 * s + t2 * c], axis=-1)


def split_heads(t, n_heads):
    b, s, _ = t.shape
    return t.reshape(b, s, n_heads, -1).transpose(0, 2, 1, 3)


def merge_heads(t):
    b, h, s, d = t.shape
    return t.transpose(0, 2, 1, 3).reshape(b, s, h * d)


def moba_attention(q, k, v):
    b, h, s, d = q.shape
    n_blocks = -(-s // MOBA_BLOCK)
    s_pad = n_blocks * MOBA_BLOCK
    topk = min(MOBA_TOPK, n_blocks)
    pad = ((0, 0), (0, 0), (0, s_pad - s), (0, 0))
    k_p = jnp.pad(k, pad)
    v_p = jnp.pad(v, pad)
    k_blocks = k_p.reshape(b, h, n_blocks, MOBA_BLOCK, d)
    v_blocks = v_p.reshape(b, h, n_blocks, MOBA_BLOCK, d)
    k_mean = jnp.mean(k_blocks.astype(jnp.float32), axis=3).astype(k.dtype)
    scale = d ** -0.5
    bi = jnp.arange(b)[:, None, None, None]
    hi = jnp.arange(h)[None, :, None, None]
    blk_ids = jnp.arange(n_blocks)
    key_off = jnp.arange(MOBA_BLOCK)

    def chunk(c):
        q0 = c * MOBA_Q_CHUNK
        qc = lax.dynamic_slice_in_dim(q, q0, MOBA_Q_CHUNK, axis=2)
        q_pos = q0 + jnp.arange(MOBA_Q_CHUNK)
        own = q0 // MOBA_BLOCK
        gate = jnp.einsum('bhqd,bhnd->bhqn', qc, k_mean).astype(jnp.float32)
        gate = jnp.where(blk_ids < own, gate, -jnp.inf)
        _, sel = lax.top_k(gate, topk)
        sel_valid = sel < own
        k_sel = k_blocks[bi, hi, sel]
        v_sel = v_blocks[bi, hi, sel]
        s_sel = jnp.einsum('bhqd,bhqnkd->bhqnk', qc, k_sel).astype(jnp.float32) * scale
        s_sel = jnp.where(sel_valid[..., None], s_sel, -jnp.inf).reshape(b, h, MOBA_Q_CHUNK, topk * MOBA_BLOCK)
        k_own = lax.dynamic_slice_in_dim(k_p, own * MOBA_BLOCK, MOBA_BLOCK, axis=2)
        v_own = lax.dynamic_slice_in_dim(v_p, own * MOBA_BLOCK, MOBA_BLOCK, axis=2)
        s_own = jnp.einsum('bhqd,bhkd->bhqk', qc, k_own).astype(jnp.float32) * scale
        causal = (own * MOBA_BLOCK + key_off)[None, :] <= q_pos[:, None]
        s_own = jnp.where(causal, s_own, -jnp.inf)
        p = jax.nn.softmax(jnp.concatenate([s_sel, s_own], axis=-1), axis=-1).astype(v.dtype)
        p_sel = p[..., :topk * MOBA_BLOCK].reshape(b, h, MOBA_Q_CHUNK, topk, MOBA_BLOCK)
        p_own = p[..., topk * MOBA_BLOCK:]
        return (jnp.einsum('bhqnk,bhqnkd->bhqd', p_sel, v_sel)
                + jnp.einsum('bhqk,bhkd->bhqd', p_own, v_own))

    outs = lax.map(chunk, jnp.arange(s // MOBA_Q_CHUNK))
    return outs.transpose(1, 2, 0, 3, 4).reshape(b, h, s, d)


def diff_attention(q1, q2, k1, k2, v, lam):
    b, h, s, dq = q1.shape
    scale = dq ** -0.5
    k_pos = jnp.arange(s)

    def block(i):
        q0 = i * DIFF_Q_BLOCK
        q_pos = q0 + jnp.arange(DIFF_Q_BLOCK)
        causal = k_pos[None, :] <= q_pos[:, None]

        def probs(qf, kf):
            qb = lax.dynamic_slice_in_dim(qf, q0, DIFF_Q_BLOCK, axis=2)
            sc = jnp.einsum('bhqd,bhkd->bhqk', qb, kf).astype(jnp.float32) * scale
            return jax.nn.softmax(jnp.where(causal, sc, -jnp.inf), axis=-1)

        p = probs(q1, k1) - lam * probs(q2, k2)
        return jnp.einsum('bhqk,bhkd->bhqd', p.astype(v.dtype), v)

    outs = lax.map(block, jnp.arange(s // DIFF_Q_BLOCK))
    return outs.transpose(1, 2, 0, 3, 4).reshape(b, h, s, -1)


def peer_ffn(x, w_query, subkeys, u_table, v_table):
    b, s, d = x.shape
    q = (x @ w_query).reshape(b, s, PEER_HEADS, 2, PEER_HALF)
    scores = jnp.einsum('bshcd,chnd->bshcn', q, subkeys).astype(jnp.float32)
    top_s, top_i = lax.top_k(scores, PEER_TOPK)
    cand = top_s[..., 0, :, None] + top_s[..., 1, None, :]
    cand_s, cand_c = lax.top_k(cand.reshape(b, s, PEER_HEADS, PEER_TOPK * PEER_TOPK), PEER_TOPK)
    i1 = jnp.take_along_axis(top_i[..., 0, :], cand_c // PEER_TOPK, axis=-1)
    i2 = jnp.take_along_axis(top_i[..., 1, :], cand_c % PEER_TOPK, axis=-1)
    experts = i1 * PEER_N_KEYS + i2
    gates = jax.nn.softmax(cand_s, axis=-1).astype(x.dtype)
    n_tok = b * s
    n_sel = PEER_HEADS * PEER_TOPK
    n_chunks = n_tok // PEER_TOKEN_CHUNK
    xs = x.reshape(n_chunks, PEER_TOKEN_CHUNK, d)
    es = experts.reshape(n_chunks, PEER_TOKEN_CHUNK, n_sel)
    gs = gates.reshape(n_chunks, PEER_TOKEN_CHUNK, n_sel)

    def chunk(args):
        xc, ec, gc = args
        hidden = jnp.einsum('td,tkd->tk', xc, u_table[ec])
        return jnp.einsum('tk,tkd->td', gc * jax.nn.gelu(hidden), v_table[ec])

    return lax.map(chunk, (xs, es, gs)).reshape(b, s, d)


def setup_inputs(seed: int = 0) -> dict:
    key = jax.random.key(seed)
    ks = jax.random.split(key, 20)
    f32 = jnp.float32
    beta = DEEPNORM_BETA
    x = jax.random.normal(ks[0], (BATCH, SEQ, D_MODEL), f32)
    positions = jnp.broadcast_to(jnp.arange(SEQ, dtype=jnp.int32)[None, :], (BATCH, SEQ))
    col_scale = jnp.concatenate([
        jnp.full((w,), beta if i in (2, 5) else 1.0, f32) for i, w in enumerate(IN_WIDTHS)])
    w_in = jax.random.normal(ks[1], (DEPTH, D_MODEL, IN_COLS), f32) * (D_MODEL ** -0.5) * col_scale
    w_branch_a = jax.random.normal(ks[2], (DEPTH, MOBA_WIDTH, D_MODEL), f32) * (MOBA_WIDTH ** -0.5) * beta
    w_branch_b = jax.random.normal(ks[3], (DEPTH, DIFF_WIDTH, D_MODEL), f32) * (DIFF_WIDTH ** -0.5) * beta
    w_out = jax.random.normal(ks[4], (DEPTH, D_MODEL, D_MODEL), f32) * (D_MODEL ** -0.5) * beta
    diff_lambda = jax.random.normal(ks[5], (DEPTH, 4, DIFF_QK_DIM), f32) * 0.1
    diff_norm_gain = 1.0 + 0.02 * jax.random.normal(ks[6], (DEPTH, DIFF_HEADS, DIFF_V_DIM), f32)
    ln1_gain = 1.0 + 0.02 * jax.random.normal(ks[7], (DEPTH, D_MODEL), f32)
    ln1_bias = 0.02 * jax.random.normal(ks[8], (DEPTH, D_MODEL), f32)
    ln2_gain = 1.0 + 0.02 * jax.random.normal(ks[9], (DEPTH, D_MODEL), f32)
    ln2_bias = 0.02 * jax.random.normal(ks[10], (DEPTH, D_MODEL), f32)
    peer_w_query = jax.random.normal(ks[11], (DEPTH, D_MODEL, PEER_HEADS * PEER_QUERY_DIM), f32) * (D_MODEL ** -0.5)
    peer_subkeys = jax.random.normal(ks[12], (DEPTH, 2, PEER_HEADS, PEER_N_KEYS, PEER_HALF), f32) * (PEER_HALF ** -0.5)
    peer_u = jax.random.normal(ks[13], (DEPTH, PEER_N_EXPERTS, D_MODEL), f32) * (D_MODEL ** -0.5)
    peer_v = jax.random.normal(ks[14], (DEPTH, PEER_N_EXPERTS, D_MODEL), f32) * beta
    return {"x": x, "positions": positions, "w_in": w_in, "w_branch_a": w_branch_a,
            "w_branch_b": w_branch_b, "w_out": w_out, "diff_lambda": diff_lambda,
            "diff_norm_gain": diff_norm_gain, "ln1_gain": ln1_gain, "ln1_bias": ln1_bias,
            "ln2_gain": ln2_gain, "ln2_bias": ln2_bias, "peer_w_query": peer_w_query,
            "peer_subkeys": peer_subkeys, "peer_u": peer_u, "peer_v": peer_v}


def reference(x, positions, w_in, w_branch_a, w_branch_b, w_out, diff_lambda, diff_norm_gain,
              ln1_gain, ln1_bias, ln2_gain, ln2_bias, peer_w_query, peer_subkeys, peer_u, peer_v):
    cos_a, sin_a = rope_tables(positions, MOBA_HEAD_DIM)
    cos_b, sin_b = rope_tables(positions, DIFF_QK_DIM)
    for layer in range(DEPTH):
        lambda_init = 0.8 - 0.6 * math.exp(-0.3 * layer)
        proj = x @ w_in[layer]
        mq, mk, mv, dq, dk, dv, gate_a, gate_b = jnp.split(proj, IN_SPLITS, axis=-1)
        qa = apply_rope(split_heads(mq, MOBA_HEADS), cos_a, sin_a)
        ka = apply_rope(split_heads(mk, MOBA_HEADS), cos_a, sin_a)
        va = split_heads(mv, MOBA_HEADS)
        out_a = merge_heads(moba_attention(qa, ka, va))
        qb = split_heads(dq, DIFF_HEADS)
        kb = split_heads(dk, DIFF_HEADS)
        vb = split_heads(dv, DIFF_HEADS)
        q1 = apply_rope(qb[..., :DIFF_QK_DIM], cos_b, sin_b)
        q2 = apply_rope(qb[..., DIFF_QK_DIM:], cos_b, sin_b)
        k1 = apply_rope(kb[..., :DIFF_QK_DIM], cos_b, sin_b)
        k2 = apply_rope(kb[..., DIFF_QK_DIM:], cos_b, sin_b)
        lam_p = diff_lambda[layer].astype(jnp.float32)
        lam = (jnp.exp(jnp.sum(lam_p[0] * lam_p[1])) - jnp.exp(jnp.sum(lam_p[2] * lam_p[3])) + lambda_init)
        ob = diff_attention(q1, q2, k1, k2, vb, lam).astype(jnp.float32)
        ob = ob * lax.rsqrt(jnp.mean(jnp.square(ob), axis=-1, keepdims=True) + RMS_EPS)
        ob = ob * diff_norm_gain[layer][None, :, None, :].astype(jnp.float32) * (1.0 - lambda_init)
        out_b = merge_heads(ob.astype(x.dtype))
        merged = (jax.nn.sigmoid(gate_a) * (out_a @ w_branch_a[layer])
                  + jax.nn.sigmoid(gate_b) * (out_b @ w_branch_b[layer]))
        x = layer_norm(DEEPNORM_ALPHA * x + merged @ w_out[layer], ln1_gain[layer], ln1_bias[layer])
        y = peer_ffn(x, peer_w_query[layer], peer_subkeys[layer], peer_u[layer], peer_v[layer])
        x = layer_norm(DEEPNORM_ALPHA * x + y, ln2_gain[layer], ln2_bias[layer])
    return x
```

```python
import functools
import math

import jax
import jax.numpy as jnp
from jax import lax
from jax.experimental import pallas as pl
from jax.experimental.pallas import tpu as pltpu

F32 = jnp.float32
BF16 = jnp.bfloat16

D_MODEL = 2048
DEPTH = 2
MOBA_HEADS = 8
MOBA_HEAD_DIM = 128
MOBA_BLOCK = 256
MOBA_TOPK = 3
MOBA_WIDTH = MOBA_HEADS * MOBA_HEAD_DIM
DIFF_HEADS = 8
DIFF_QK_DIM = 64
DIFF_V_DIM = 2 * DIFF_QK_DIM
DIFF_WIDTH = DIFF_HEADS * DIFF_V_DIM
IN_COLS = 3 * MOBA_WIDTH + 3 * DIFF_WIDTH + 2 * D_MODEL
PEER_HEADS = 8
PEER_N_KEYS = 128
PEER_N_EXPERTS = PEER_N_KEYS * PEER_N_KEYS
PEER_TOPK = 16
PEER_QUERY_DIM = 256
PEER_HALF = PEER_QUERY_DIM // 2
ROPE_THETA = 10000.0
LN_EPS = 1e-5
RMS_EPS = 1e-6
DEEPNORM_ALPHA = (2 * DEPTH) ** 0.25

LANES = 128
VMEM_LIMIT = 56 * 1024 * 1024
NEG_BIG = -1e30

NT_DIMS = (((1,), (1,)), ((), ()))


def _params(*sem):
    return pltpu.CompilerParams(dimension_semantics=sem, vmem_limit_bytes=VMEM_LIMIT)


PROJ_TM = 1024
PROJ_TN = 1024


def _rope_full(t, cos, sin_signed):
    return t * cos + pltpu.roll(t, LANES // 2, 1) * sin_signed


def _rope_pairs(t, cos, sin_signed):
    lane = lax.broadcasted_iota(jnp.int32, t.shape, 1)
    first_half = (lane % DIFF_QK_DIM) < (DIFF_QK_DIM // 2)
    partner = jnp.where(first_half, pltpu.roll(t, LANES - DIFF_QK_DIM // 2, 1), pltpu.roll(t, DIFF_QK_DIM // 2, 1))
    return t * cos + partner * sin_signed


def _in_proj_kernel(x_ref, w_ref, cosa_ref, sina_ref, cosb_ref, sinb_ref, o_ref, xb_ref):
    j = pl.program_id(1)

    @pl.when(j == 0)
    def _():
        xb_ref[...] = x_ref[...].astype(BF16)

    acc = jnp.dot(xb_ref[...], w_ref[...], preferred_element_type=F32)
    n_heads = acc.shape[1] // LANES
    moba_qk = j < 2
    diff_qk = (j == 3) | (j == 4)
    plain = (j == 2) | (j == 5)

    def store(fn):
        for h in range(n_heads):
            sl = slice(h * LANES, (h + 1) * LANES)
            o_ref[:, sl] = fn(acc[:, sl]).astype(o_ref.dtype)

    @pl.when(moba_qk)
    def _():
        store(lambda t: _rope_full(t, cosa_ref[...], sina_ref[...]))

    @pl.when(diff_qk)
    def _():
        store(lambda t: _rope_pairs(t, cosb_ref[...], sinb_ref[...]))

    @pl.when(plain)
    def _():
        store(lambda t: t)

    @pl.when(j >= 6)
    def _():
        store(jax.nn.sigmoid)


def _in_proj(x, w_bf16, tables):
    t, d = x.shape
    n = w_bf16.shape[1]
    tm = min(PROJ_TM, t)
    tab_spec = pl.BlockSpec((tm, LANES), lambda i, j: (i, 0))
    return pl.pallas_call(
        _in_proj_kernel,
        grid=(t // tm, n // PROJ_TN),
        in_specs=[pl.BlockSpec((tm, d), lambda i, j: (i, 0)),
                  pl.BlockSpec((d, PROJ_TN), lambda i, j: (0, j)),
                  tab_spec, tab_spec, tab_spec, tab_spec],
        out_specs=pl.BlockSpec((tm, PROJ_TN), lambda i, j: (i, j)),
        out_shape=jax.ShapeDtypeStruct((t, n), BF16),
        scratch_shapes=[pltpu.VMEM((tm, d), BF16)],
        compiler_params=_params("parallel", "arbitrary"),
        name="in_proj",
    )(x, w_bf16, *tables)


def _softmax_step(s, m_prev, l_prev, acc_prev, vt):
    m_new = jnp.maximum(m_prev, jnp.max(s, axis=0, keepdims=True))
    alpha = jnp.exp(m_prev - m_new)
    p = jnp.exp(s - m_new)
    l_new = alpha * l_prev + jnp.sum(p, axis=0, keepdims=True)
    acc_new = alpha * acc_prev + jnp.dot(vt, p.astype(BF16), preferred_element_type=F32)
    return m_new, l_new, acc_new


def _store_vt(v_ref, vt_ref, blk):
    for n in range(vt_ref.shape[0]):
        vt_ref[n] = v_ref[n * blk:(n + 1) * blk, :].astype(F32).T.astype(BF16)


def _moba_kernel(q_ref, k_ref, v_ref, o_ref, kmean_ref, vt_ref, sel_ref, *, blk, topk, scale):
    i = pl.program_id(2)
    n_blocks = vt_ref.shape[0]
    d = q_ref.shape[1]

    @pl.when(i == 0)
    def _():
        for n in range(n_blocks):
            kb = k_ref[n * blk:(n + 1) * blk, :].astype(F32)
            kmean_ref[n:n + 1, :] = jnp.mean(kb, axis=0, keepdims=True)
        _store_vt(v_ref, vt_ref, blk)

    q = q_ref[...]
    gate = lax.dot_general(kmean_ref[...].astype(BF16), q, NT_DIMS, preferred_element_type=F32)
    rows = lax.broadcasted_iota(jnp.int32, gate.shape, 0)
    past = rows < i
    g = jnp.where(past, gate, -jnp.inf)
    sel = jnp.zeros(gate.shape, F32)
    for _ in range(topk):
        m = jnp.max(g, axis=0, keepdims=True)
        idx = jnp.min(jnp.where(g == m, rows, n_blocks), axis=0, keepdims=True)
        pick = rows == idx
        sel = jnp.where(pick, 1.0, sel)
        g = jnp.where(pick, -jnp.inf, g)
    sel_ref[...] = jnp.where(past, sel, 0.0)

    def past_block(j, carry):
        kj = k_ref[pl.ds(pl.multiple_of(j * blk, blk), blk), :]
        s = lax.dot_general(kj, q, NT_DIMS, preferred_element_type=F32) * scale
        s = jnp.where(sel_ref[pl.ds(j, 1), :] > 0.0, s, -jnp.inf)
        return _softmax_step(s, *carry, vt_ref[j])

    init = (jnp.full((1, blk), NEG_BIG, F32), jnp.zeros((1, blk), F32), jnp.zeros((d, blk), F32))
    carry = lax.fori_loop(0, i, past_block, init)

    ki = k_ref[pl.ds(pl.multiple_of(i * blk, blk), blk), :]
    s = lax.dot_general(ki, q, NT_DIMS, preferred_element_type=F32) * scale
    key_pos = lax.broadcasted_iota(jnp.int32, s.shape, 0)
    q_pos = lax.broadcasted_iota(jnp.int32, s.shape, 1)
    s = jnp.where(key_pos <= q_pos, s, -jnp.inf)
    _, l, acc = _softmax_step(s, *carry, vt_ref[i])
    o_ref[...] = (acc / l).T.astype(o_ref.dtype)


def _moba(proj, batch, seq):
    blk = MOBA_BLOCK
    d = MOBA_HEAD_DIM
    n_blocks = seq // blk
    kern = functools.partial(_moba_kernel, blk=blk, topk=min(MOBA_TOPK, n_blocks), scale=d ** -0.5)
    return pl.pallas_call(
        kern,
        grid=(batch, MOBA_HEADS, n_blocks),
        in_specs=[pl.BlockSpec((blk, d), lambda b, h, i: (b * n_blocks + i, h)),
                  pl.BlockSpec((seq, d), lambda b, h, i: (b, MOBA_HEADS + h)),
                  pl.BlockSpec((seq, d), lambda b, h, i: (b, 2 * MOBA_HEADS + h))],
        out_specs=pl.BlockSpec((blk, d), lambda b, h, i: (b * n_blocks + i, h)),
        out_shape=jax.ShapeDtypeStruct((batch * seq, MOBA_WIDTH), BF16),
        scratch_shapes=[pltpu.VMEM((n_blocks, d), F32),
                        pltpu.VMEM((n_blocks, d, blk), BF16),
                        pltpu.VMEM((n_blocks, blk), F32)],
        compiler_params=_params("parallel", "parallel", "arbitrary"),
        name="moba_attn",
    )(proj, proj, proj)


DIFF_TILE = 256


def _diff_kernel(lam_ref, gain_ref, q_ref, k_ref, v_ref, o_ref, vt_ref, *, tile, scale, lambda_init):
    i = pl.program_id(2)
    d = v_ref.shape[1]

    @pl.when(i == 0)
    def _():
        _store_vt(v_ref, vt_ref, tile)

    q = q_ref[...]
    lane = lax.broadcasted_iota(jnp.int32, q.shape, 1)
    zero = jnp.zeros_like(q)
    q1 = jnp.where(lane < DIFF_QK_DIM, q, zero)
    q2 = jnp.where(lane < DIFF_QK_DIM, zero, q)

    def scores(j):
        kj = k_ref[pl.ds(pl.multiple_of(j * tile, tile), tile), :]
        s1 = lax.dot_general(kj, q1, NT_DIMS, preferred_element_type=F32) * scale
        s2 = lax.dot_general(kj, q2, NT_DIMS, preferred_element_type=F32) * scale
        return s1, s2

    def past_tile(j, carry):
        c1, c2 = carry
        s1, s2 = scores(j)
        vt = vt_ref[j]
        return _softmax_step(s1, *c1, vt), _softmax_step(s2, *c2, vt)

    def init():
        return (jnp.full((1, tile), NEG_BIG, F32), jnp.zeros((1, tile), F32), jnp.zeros((d, tile), F32))

    c1, c2 = lax.fori_loop(0, i, past_tile, (init(), init()))

    s1, s2 = scores(i)
    key_pos = lax.broadcasted_iota(jnp.int32, s1.shape, 0)
    q_pos = lax.broadcasted_iota(jnp.int32, s1.shape, 1)
    causal = key_pos <= q_pos
    vt = vt_ref[i]
    _, l1, a1 = _softmax_step(jnp.where(causal, s1, -jnp.inf), *c1, vt)
    _, l2, a2 = _softmax_step(jnp.where(causal, s2, -jnp.inf), *c2, vt)

    lp = lam_ref[...]
    lam = (jnp.exp(jnp.sum(lp[0:1] * lp[1:2], axis=1, keepdims=True))
           - jnp.exp(jnp.sum(lp[2:3] * lp[3:4], axis=1, keepdims=True)) + lambda_init)
    o = (a1 / l1 - lam * (a2 / l2)).T
    o = o * lax.rsqrt(jnp.mean(jnp.square(o), axis=1, keepdims=True) + RMS_EPS)
    o = o * gain_ref[0] * (1.0 - lambda_init)
    o_ref[...] = o.astype(o_ref.dtype)


def _diff(proj, lam_p, gain, batch, seq, lambda_init):
    tile = DIFF_TILE
    d = DIFF_V_DIM
    n_tiles = seq // tile
    col0 = 3 * MOBA_WIDTH // d
    kern = functools.partial(_diff_kernel, tile=tile, scale=DIFF_QK_DIM ** -0.5, lambda_init=lambda_init)
    return pl.pallas_call(
        kern,
        grid=(batch, DIFF_HEADS, n_tiles),
        in_specs=[pl.BlockSpec(lam_p.shape, lambda b, h, i: (0, 0)),
                  pl.BlockSpec((1, 1, d), lambda b, h, i: (h, 0, 0)),
                  pl.BlockSpec((tile, d), lambda b, h, i: (b * n_tiles + i, col0 + h)),
                  pl.BlockSpec((seq, d), lambda b, h, i: (b, col0 + DIFF_HEADS + h)),
                  pl.BlockSpec((seq, d), lambda b, h, i: (b, col0 + 2 * DIFF_HEADS + h))],
        out_specs=pl.BlockSpec((tile, d), lambda b, h, i: (b * n_tiles + i, h)),
        out_shape=jax.ShapeDtypeStruct((batch * seq, DIFF_WIDTH), BF16),
        scratch_shapes=[pltpu.VMEM((n_tiles, d, tile), BF16)],
        compiler_params=_params("parallel", "parallel", "arbitrary"),
        name="diff_attn",
    )(lam_p, gain[:, None, :], proj, proj, proj)


MERGE_TM = 512


def _merge_kernel(oa_ref, ob_ref, ga_ref, gb_ref, wa_ref, wb_ref, o_ref):
    a = jnp.dot(oa_ref[...], wa_ref[...], preferred_element_type=F32)
    b = jnp.dot(ob_ref[...], wb_ref[...], preferred_element_type=F32)
    o_ref[...] = (ga_ref[...].astype(F32) * a + gb_ref[...].astype(F32) * b).astype(o_ref.dtype)


def _merge(out_a, out_b, proj, wa, wb):
    t = out_a.shape[0]
    tm = min(MERGE_TM, t)
    gate_col = (3 * MOBA_WIDTH + 3 * DIFF_WIDTH) // D_MODEL
    const = lambda i: (0, 0)
    return pl.pallas_call(
        _merge_kernel,
        grid=(t // tm,),
        in_specs=[pl.BlockSpec((tm, MOBA_WIDTH), lambda i: (i, 0)),
                  pl.BlockSpec((tm, DIFF_WIDTH), lambda i: (i, 0)),
                  pl.BlockSpec((tm, D_MODEL), lambda i: (i, gate_col)),
                  pl.BlockSpec((tm, D_MODEL), lambda i: (i, gate_col + 1)),
                  pl.BlockSpec(wa.shape, const, pipeline_mode=pl.Buffered(1)),
                  pl.BlockSpec(wb.shape, const, pipeline_mode=pl.Buffered(1))],
        out_specs=pl.BlockSpec((tm, D_MODEL), lambda i: (i, 0)),
        out_shape=jax.ShapeDtypeStruct((t, D_MODEL), BF16),
        compiler_params=_params("parallel"),
        name="gated_merge",
    )(out_a, out_b, proj, proj, wa, wb)


def _layer_norm(z, gain, bias):
    mu = jnp.mean(z, axis=-1, keepdims=True)
    zc = z - mu
    var = jnp.mean(jnp.square(zc), axis=-1, keepdims=True)
    return zc * lax.rsqrt(var + LN_EPS) * gain + bias


def _out_ln_kernel(x_ref, m_ref, w_ref, g_ref, b_ref, o_ref, ot_ref):
    y = jnp.dot(m_ref[...], w_ref[...], preferred_element_type=F32)
    xn = _layer_norm(DEEPNORM_ALPHA * x_ref[...] + y, g_ref[...], b_ref[...])
    o_ref[...] = xn
    ot_ref[...] = xn.T.astype(ot_ref.dtype)


def _out_ln(x, merged, w, gain, bias):
    t, d = x.shape
    tm = min(MERGE_TM, t)
    const = lambda i: (0, 0)
    return pl.pallas_call(
        _out_ln_kernel,
        grid=(t // tm,),
        in_specs=[pl.BlockSpec((tm, d), lambda i: (i, 0)),
                  pl.BlockSpec((tm, d), lambda i: (i, 0)),
                  pl.BlockSpec(w.shape, const, pipeline_mode=pl.Buffered(1)),
                  pl.BlockSpec((1, d), const),
                  pl.BlockSpec((1, d), const)],
        out_specs=[pl.BlockSpec((tm, d), lambda i: (i, 0)),
                   pl.BlockSpec((d, tm), lambda i: (0, i))],
        out_shape=[jax.ShapeDtypeStruct((t, d), F32), jax.ShapeDtypeStruct((d, t), BF16)],
        compiler_params=_params("parallel"),
        name="out_proj_ln",
    )(x, merged, w, gain, bias)


ROUTE_TT = 256


def _top_rows(s, k):
    out = []
    for _ in range(k):
        m = jnp.max(s, axis=0, keepdims=True)
        out.append(m)
        s = jnp.where(s == m, -jnp.inf, s)
    return out


def _route_kernel(xt_ref, wq_ref, keys_ref, s1_ref, a1_ref, s2_ref, a2_ref, thr_ref, sc_ref):
    n_heads, n_keys, tt = s1_ref.shape
    half = keys_ref.shape[3]
    qt = jnp.dot(wq_ref[...], xt_ref[...], preferred_element_type=F32)
    for h in range(n_heads):
        for c in range(2):
            r0 = (2 * h + c) * half
            sc_ref[2 * h + c] = jnp.dot(keys_ref[c, h], qt[r0:r0 + half, :].astype(BF16),
                                        preferred_element_type=F32)

    def one(idx, _):
        h = idx // (tt // LANES)
        lanes = pl.ds(pl.multiple_of((idx % (tt // LANES)) * LANES, LANES), LANES)
        s1 = sc_ref[2 * h, :, lanes]
        s2 = sc_ref[2 * h + 1, :, lanes]
        top1 = _top_rows(s1, PEER_TOPK)
        top2 = jnp.concatenate(_top_rows(s2, PEER_TOPK), axis=0)
        cand = [t1 + top2 for t1 in top1]
        m0 = top1[0] + top2[0:1]
        z = jnp.zeros_like(m0)
        m = m0
        for _ in range(PEER_TOPK):
            m = jnp.max(functools.reduce(jnp.maximum, cand), axis=0, keepdims=True)
            z = z + jnp.exp(m - m0)
            cand = [jnp.where(cd == m, -jnp.inf, cd) for cd in cand]
        s1_ref[h, :, lanes] = s1
        s2_ref[h, :, lanes] = s2
        a1_ref[h, :, lanes] = jnp.exp(s1 - top1[0]) / z
        a2_ref[h, :, lanes] = jnp.exp(s2 - top2[0:1])
        thr_ref[h, :, lanes] = m
        return 0

    lax.fori_loop(0, n_heads * (tt // LANES), one, 0)


def _route(xt, wq_t, subkeys):
    d, t = xt.shape
    tt = min(ROUTE_TT, t)
    big = pl.BlockSpec((PEER_HEADS, PEER_N_KEYS, tt), lambda i: (0, 0, i))
    big_shape = jax.ShapeDtypeStruct((PEER_HEADS, PEER_N_KEYS, t), F32)
    return pl.pallas_call(
        _route_kernel,
        grid=(t // tt,),
        in_specs=[pl.BlockSpec((d, tt), lambda i: (0, i)),
                  pl.BlockSpec(wq_t.shape, lambda i: (0, 0), pipeline_mode=pl.Buffered(1)),
                  pl.BlockSpec(subkeys.shape, lambda i: (0, 0, 0, 0))],
        out_specs=[big, big, big, big, pl.BlockSpec((PEER_HEADS, 1, tt), lambda i: (0, 0, i))],
        out_shape=[big_shape, big_shape, big_shape, big_shape, jax.ShapeDtypeStruct((PEER_HEADS, 1, t), F32)],
        scratch_shapes=[pltpu.VMEM((2 * PEER_HEADS, PEER_N_KEYS, tt), F32)],
        compiler_params=_params("parallel"),
        name="peer_route",
    )(xt, wq_t, subkeys)


PEER_TT = 512
PEER_EC = 1024
PEER_SUB = 256


def _peer_kernel(xt_ref, u_ref, vt_ref, s1_ref, a1_ref, s2_ref, a2_ref, thr_ref, o_ref, acc_ref, g_ref, *, n_chunks):
    c = pl.program_id(1)
    n_heads = s2_ref.shape[0]
    tt = xt_ref.shape[1]
    ec = u_ref.shape[0]

    @pl.when(c == 0)
    def _():
        acc_ref[...] = jnp.zeros_like(acc_ref)
        g_ref[1] = jnp.zeros(g_ref.shape[1:], g_ref.dtype)

    def second_matmul():
        acc_ref[...] += jnp.dot(vt_ref[...], g_ref[(c + 1) % 2], preferred_element_type=F32)

    def first_matmul_and_gate():
        xt = xt_ref[...]
        slot = c % 2
        for sub in range(ec // PEER_SUB):
            r0 = sub * PEER_SUB
            hid = jnp.dot(u_ref[r0:r0 + PEER_SUB, :], xt, preferred_element_type=F32)
            for r in range(PEER_SUB // PEER_N_KEYS):
                i1 = r0 // PEER_N_KEYS + r
                for lc in range(tt // LANES):
                    lanes = slice(lc * LANES, (lc + 1) * LANES)
                    w = jnp.zeros((PEER_N_KEYS, LANES), F32)
                    for h in range(n_heads):
                        cand = s1_ref[h, i1:i1 + 1, lanes] + s2_ref[h, :, lanes]
                        gate = a1_ref[h, i1:i1 + 1, lanes] * a2_ref[h, :, lanes]
                        w = w + jnp.where(cand >= thr_ref[h, :, lanes], gate, 0.0)
                    hb = hid[r * PEER_N_KEYS:(r + 1) * PEER_N_KEYS, lanes]
                    g_ref[slot, r0 + r * PEER_N_KEYS:r0 + (r + 1) * PEER_N_KEYS, lanes] = (
                        jax.nn.gelu(hb, approximate=True) * w).astype(g_ref.dtype)

    @pl.when(c < n_chunks)
    def _():
        second_matmul()
        first_matmul_and_gate()

    @pl.when(c == n_chunks)
    def _():
        second_matmul()
        o_ref[...] = acc_ref[...]


def _peer(xt, u_bf16, vt_bf16, s1, a1, s2, a2, thr):
    d, t = xt.shape
    n_exp = u_bf16.shape[0]
    tt = min(PEER_TT, t)
    ec = PEER_EC
    n_chunks = n_exp // ec
    rows = ec // PEER_N_KEYS
    last = n_chunks - 1
    cur = lambda c: jnp.minimum(c, last)
    prev = lambda c: jnp.maximum(c - 1, 0)
    chunk = pl.BlockSpec((PEER_HEADS, rows, tt), lambda i, c: (0, cur(c), i))
    full = pl.BlockSpec((PEER_HEADS, PEER_N_KEYS, tt), lambda i, c: (0, 0, i))
    kern = functools.partial(_peer_kernel, n_chunks=n_chunks)
    return pl.pallas_call(
        kern,
        grid=(t // tt, n_chunks + 1),
        in_specs=[pl.BlockSpec((d, tt), lambda i, c: (0, i)),
                  pl.BlockSpec((ec, d), lambda i, c: (cur(c), 0)),
                  pl.BlockSpec((d, ec), lambda i, c: (0, prev(c))),
                  chunk, chunk, full, full,
                  pl.BlockSpec((PEER_HEADS, 1, tt), lambda i, c: (0, 0, i))],
        out_specs=pl.BlockSpec((d, tt), lambda i, c: (0, i)),
        out_shape=jax.ShapeDtypeStruct((d, t), F32),
        scratch_shapes=[pltpu.VMEM((d, tt), F32), pltpu.VMEM((2, ec, tt), BF16)],
        compiler_params=_params("parallel", "arbitrary"),
        name="peer_dense",
    )(xt, u_bf16, vt_bf16, s1, a1, s2, a2, thr)


LN2_TM = 512


def _ln2_kernel(x_ref, yt_ref, g_ref, b_ref, o_ref):
    o_ref[...] = _layer_norm(DEEPNORM_ALPHA * x_ref[...] + yt_ref[...].T, g_ref[...], b_ref[...])


def _ln2(x, yt, gain, bias):
    t, d = x.shape
    tm = min(LN2_TM, t)
    const = lambda i: (0, 0)
    return pl.pallas_call(
        _ln2_kernel,
        grid=(t // tm,),
        in_specs=[pl.BlockSpec((tm, d), lambda i: (i, 0)),
                  pl.BlockSpec((d, tm), lambda i: (0, i)),
                  pl.BlockSpec((1, d), const),
                  pl.BlockSpec((1, d), const)],
        out_specs=pl.BlockSpec((tm, d), lambda i: (i, 0)),
        out_shape=jax.ShapeDtypeStruct((t, d), F32),
        compiler_params=_params("parallel"),
        name="residual_ln2",
    )(x, yt, gain, bias)


def _rope_tables(positions):
    pos = positions.reshape(-1).astype(F32)[:, None]

    def table(dim):
        inv_freq = ROPE_THETA ** (-jnp.arange(0, dim, 2, dtype=F32) / dim)
        ang = pos * inv_freq
        return jnp.cos(ang), jnp.sin(ang)

    ca, sa = table(MOBA_HEAD_DIM)
    cb, sb = table(DIFF_QK_DIM)
    return (jnp.concatenate([ca, ca], axis=1), jnp.concatenate([-sa, sa], axis=1),
            jnp.concatenate([cb, cb, cb, cb], axis=1), jnp.concatenate([-sb, sb, -sb, sb], axis=1))


def kernel(x, positions, w_in, w_branch_a, w_branch_b, w_out, diff_lambda, diff_norm_gain, ln1_gain, ln1_bias,
           ln2_gain, ln2_bias, peer_w_query, peer_subkeys, peer_u, peer_v):
    batch, seq, d = x.shape
    xf = x.reshape(batch * seq, d)
    tables = _rope_tables(positions)
    for layer in range(DEPTH):
        lambda_init = 0.8 - 0.6 * math.exp(-0.3 * layer)
        proj = _in_proj(xf, w_in[layer].astype(BF16), tables)
        out_a = _moba(proj, batch, seq)
        out_b = _diff(proj, diff_lambda[layer], diff_norm_gain[layer], batch, seq, lambda_init)
        merged = _merge(out_a, out_b, proj, w_branch_a[layer].astype(BF16), w_branch_b[layer].astype(BF16))
        x1, x1t = _out_ln(xf, merged, w_out[layer].astype(BF16), ln1_gain[layer][None], ln1_bias[layer][None])
        s1, a1, s2, a2, thr = _route(x1t, peer_w_query[layer].T.astype(BF16), peer_subkeys[layer].astype(BF16))
        yt = _peer(x1t, peer_u[layer].astype(BF16), peer_v[layer].T.astype(BF16), s1, a1, s2, a2, thr)
        xf = _ln2(x1, yt, ln2_gain[layer][None], ln2_bias[layer][None])
    return xf.reshape(batch, seq, d)
```

```python
import functools
import math

import jax
import jax.numpy as jnp
from jax import lax
from jax.experimental import pallas as pl
from jax.experimental.pallas import tpu as pltpu

F32 = jnp.float32
BF16 = jnp.bfloat16

D_MODEL = 2048
DEPTH = 2
MOBA_HEADS = 8
MOBA_HEAD_DIM = 128
MOBA_BLOCK = 256
MOBA_TOPK = 3
MOBA_WIDTH = MOBA_HEADS * MOBA_HEAD_DIM
DIFF_HEADS = 8
DIFF_QK_DIM = 64
DIFF_V_DIM = 2 * DIFF_QK_DIM
DIFF_WIDTH = DIFF_HEADS * DIFF_V_DIM
IN_COLS = 3 * MOBA_WIDTH + 3 * DIFF_WIDTH + 2 * D_MODEL
PEER_HEADS = 8
PEER_N_KEYS = 128
PEER_N_EXPERTS = PEER_N_KEYS * PEER_N_KEYS
PEER_TOPK = 16
PEER_QUERY_DIM = 256
PEER_HALF = PEER_QUERY_DIM // 2
ROPE_THETA = 10000.0
LN_EPS = 1e-5
RMS_EPS = 1e-6
DEEPNORM_ALPHA = (2 * DEPTH) ** 0.25

LANES = 128
VMEM_LIMIT = 56 * 1024 * 1024
NEG_BIG = -1e30

NT_DIMS = (((1,), (1,)), ((), ()))


def _params(*sem):
    return pltpu.CompilerParams(dimension_semantics=sem, vmem_limit_bytes=VMEM_LIMIT)


PROJ_TM = 1024
PROJ_TN = 1024


def _rope_full(t, cos, sin_signed):
    return t * cos + pltpu.roll(t, LANES // 2, 1) * sin_signed


def _rope_pairs(t, cos, sin_signed):
    lane = lax.broadcasted_iota(jnp.int32, t.shape, 1)
    first_half = (lane % DIFF_QK_DIM) < (DIFF_QK_DIM // 2)
    partner = jnp.where(first_half, pltpu.roll(t, LANES - DIFF_QK_DIM // 2, 1), pltpu.roll(t, DIFF_QK_DIM // 2, 1))
    return t * cos + partner * sin_signed


def _in_proj_kernel(x_ref, w_ref, cosa_ref, sina_ref, cosb_ref, sinb_ref, o_ref, xb_ref):
    j = pl.program_id(1)

    @pl.when(j == 0)
    def _():
        xb_ref[...] = x_ref[...].astype(BF16)

    acc = jnp.dot(xb_ref[...], w_ref[...], preferred_element_type=F32)
    n_heads = acc.shape[1] // LANES
    moba_qk = j < 2
    diff_qk = (j == 3) | (j == 4)
    plain = (j == 2) | (j == 5)

    def store(fn):
        for h in range(n_heads):
            sl = slice(h * LANES, (h + 1) * LANES)
            o_ref[:, sl] = fn(acc[:, sl]).astype(o_ref.dtype)

    @pl.when(moba_qk)
    def _():
        store(lambda t: _rope_full(t, cosa_ref[...], sina_ref[...]))

    @pl.when(diff_qk)
    def _():
        store(lambda t: _rope_pairs(t, cosb_ref[...], sinb_ref[...]))

    @pl.when(plain)
    def _():
        store(lambda t: t)

    @pl.when(j >= 6)
    def _():
        store(jax.nn.sigmoid)


def _in_proj(x, w_bf16, tables):
    t, d = x.shape
    n = w_bf16.shape[1]
    tm = min(PROJ_TM, t)
    tab_spec = pl.BlockSpec((tm, LANES), lambda i, j: (i, 0))
    return pl.pallas_call(
        _in_proj_kernel,
        grid=(t // tm, n // PROJ_TN),
        in_specs=[pl.BlockSpec((tm, d), lambda i, j: (i, 0)),
                  pl.BlockSpec((d, PROJ_TN), lambda i, j: (0, j)),
                  tab_spec, tab_spec, tab_spec, tab_spec],
        out_specs=pl.BlockSpec((tm, PROJ_TN), lambda i, j: (i, j)),
        out_shape=jax.ShapeDtypeStruct((t, n), BF16),
        scratch_shapes=[pltpu.VMEM((tm, d), BF16)],
        compiler_params=_params("parallel", "arbitrary"),
        name="in_proj",
    )(x, w_bf16, *tables)


LOG2E = math.log2(math.e)


def _softmax_step(slabs, m_prev, l_prev, acc_prev, vt, c):
    m_cur = functools.reduce(jnp.maximum, [jnp.max(s, axis=0, keepdims=True) for s in slabs])
    m_new = jnp.maximum(m_prev, m_cur)
    alpha = jnp.exp2((m_prev - m_new) * c)
    ps = [jnp.exp2((s - m_new) * c) for s in slabs]
    l_new = alpha * l_prev + functools.reduce(jnp.add, [jnp.sum(p, axis=0, keepdims=True) for p in ps])
    p_all = ps[0] if len(ps) == 1 else jnp.concatenate(ps, axis=0)
    acc_new = alpha * acc_prev + jnp.dot(vt, p_all.astype(BF16), preferred_element_type=F32)
    return m_new, l_new, acc_new


def _softmax_init(d, n_queries):
    return (jnp.full((1, n_queries), NEG_BIG, F32), jnp.zeros((1, n_queries), F32), jnp.zeros((d, n_queries), F32))


def _causal(s):
    key_pos = lax.broadcasted_iota(jnp.int32, s.shape, 0)
    q_pos = lax.broadcasted_iota(jnp.int32, s.shape, 1)
    return jnp.where(key_pos <= q_pos, s, -jnp.inf)


def _transposed(v_ref, rows):
    return v_ref[rows, :].astype(F32).T.astype(BF16)


MOBA_GROUP = 4


def _moba_kernel(q_ref, k_ref, v_ref, o_ref, kmean_ref, vtb_ref, vtg_ref, sel_ref, *, blk, group, topk, c):
    i = pl.program_id(2)
    n_blocks = vtb_ref.shape[0]
    d = q_ref.shape[1]

    @pl.when(i == 0)
    def _():
        for n in range(n_blocks):
            rows = slice(n * blk, (n + 1) * blk)
            kmean_ref[n:n + 1, :] = jnp.mean(k_ref[rows, :].astype(F32), axis=0, keepdims=True)
            vt = _transposed(v_ref, rows)
            vtb_ref[n] = vt
            vtg_ref[n // group, :, (n % group) * blk:(n % group + 1) * blk] = vt

    q = q_ref[...]
    gate = lax.dot_general(kmean_ref[...].astype(BF16), q, NT_DIMS, preferred_element_type=F32)
    rows = lax.broadcasted_iota(jnp.int32, gate.shape, 0)
    past = rows < i
    g = jnp.where(past, gate, -jnp.inf)
    sel = jnp.zeros(gate.shape, F32)
    for _ in range(topk):
        m = jnp.max(g, axis=0, keepdims=True)
        idx = jnp.min(jnp.where(g == m, rows, n_blocks), axis=0, keepdims=True)
        pick = rows == idx
        sel = jnp.where(pick, 1.0, sel)
        g = jnp.where(pick, -jnp.inf, g)
    sel_ref[...] = jnp.where(past, sel, 0.0)

    def past_group(gi, carry):
        kg = k_ref[pl.ds(pl.multiple_of(gi * (group * blk), group * blk), group * blk), :]
        s = lax.dot_general(kg, q, NT_DIMS, preferred_element_type=F32)
        slabs = [jnp.where(sel_ref[pl.ds(gi * group + b, 1), :] > 0.0, s[b * blk:(b + 1) * blk], -jnp.inf)
                 for b in range(group)]
        return _softmax_step(slabs, *carry, vtg_ref[gi], c)

    carry = lax.fori_loop(0, (i + group - 1) // group, past_group, _softmax_init(d, blk))

    ki = k_ref[pl.ds(pl.multiple_of(i * blk, blk), blk), :]
    s = _causal(lax.dot_general(ki, q, NT_DIMS, preferred_element_type=F32))
    _, l, acc = _softmax_step([s], *carry, vtb_ref[i], c)
    o_ref[...] = (acc / l).T.astype(o_ref.dtype)


def _moba(proj, batch, seq):
    blk = MOBA_BLOCK
    d = MOBA_HEAD_DIM
    n_blocks = seq // blk
    group = math.gcd(MOBA_GROUP, n_blocks)
    kern = functools.partial(_moba_kernel, blk=blk, group=group, topk=min(MOBA_TOPK, n_blocks),
                             c=d ** -0.5 * LOG2E)
    return pl.pallas_call(
        kern,
        grid=(batch, MOBA_HEADS, n_blocks),
        in_specs=[pl.BlockSpec((blk, d), lambda b, h, i: (b * n_blocks + i, h)),
                  pl.BlockSpec((seq, d), lambda b, h, i: (b, MOBA_HEADS + h)),
                  pl.BlockSpec((seq, d), lambda b, h, i: (b, 2 * MOBA_HEADS + h))],
        out_specs=pl.BlockSpec((blk, d), lambda b, h, i: (b * n_blocks + i, h)),
        out_shape=jax.ShapeDtypeStruct((batch * seq, MOBA_WIDTH), BF16),
        scratch_shapes=[pltpu.VMEM((n_blocks, d), F32),
                        pltpu.VMEM((n_blocks, d, blk), BF16),
                        pltpu.VMEM((n_blocks // group, d, group * blk), BF16),
                        pltpu.VMEM((n_blocks, blk), F32)],
        compiler_params=_params("parallel", "parallel", "arbitrary"),
        name="moba_attn",
    )(proj, proj, proj)


DIFF_TILE = 512


def _diff_kernel(lam_ref, gain_ref, q_ref, k_ref, v_ref, o_ref, vt_ref, *, tile, scale, lambda_init):
    i = pl.program_id(2)
    d = v_ref.shape[1]

    @pl.when(i == 0)
    def _():
        for n in range(vt_ref.shape[0]):
            vt_ref[n] = _transposed(v_ref, slice(n * tile, (n + 1) * tile))

    q = q_ref[...] * scale
    lane = lax.broadcasted_iota(jnp.int32, q.shape, 1)
    zero = jnp.zeros_like(q)
    q1 = jnp.where(lane < DIFF_QK_DIM, q, zero)
    q2 = jnp.where(lane < DIFF_QK_DIM, zero, q)

    def scores(j):
        kj = k_ref[pl.ds(pl.multiple_of(j * tile, tile), tile), :]
        return (lax.dot_general(kj, q1, NT_DIMS, preferred_element_type=F32),
                lax.dot_general(kj, q2, NT_DIMS, preferred_element_type=F32))

    def past_tile(j, carry):
        c1, c2 = carry
        s1, s2 = scores(j)
        vt = vt_ref[j]
        return _softmax_step([s1], *c1, vt, LOG2E), _softmax_step([s2], *c2, vt, LOG2E)

    c1, c2 = lax.fori_loop(0, i, past_tile, (_softmax_init(d, tile), _softmax_init(d, tile)))

    s1, s2 = scores(i)
    vt = vt_ref[i]
    _, l1, a1 = _softmax_step([_causal(s1)], *c1, vt, LOG2E)
    _, l2, a2 = _softmax_step([_causal(s2)], *c2, vt, LOG2E)

    lp = lam_ref[...]
    lam = (jnp.exp(jnp.sum(lp[0:1] * lp[1:2], axis=1, keepdims=True))
           - jnp.exp(jnp.sum(lp[2:3] * lp[3:4], axis=1, keepdims=True)) + lambda_init)
    o = (a1 / l1 - lam * (a2 / l2)).T
    o = o * lax.rsqrt(jnp.mean(jnp.square(o), axis=1, keepdims=True) + RMS_EPS)
    o = o * gain_ref[0] * (1.0 - lambda_init)
    o_ref[...] = o.astype(o_ref.dtype)


def _diff(proj, lam_p, gain, batch, seq, lambda_init):
    tile = min(DIFF_TILE, seq)
    d = DIFF_V_DIM
    n_tiles = seq // tile
    col0 = 3 * MOBA_WIDTH // d
    scale = DIFF_QK_DIM ** -0.5
    assert math.frexp(scale)[0] == 0.5, "the score scale is folded into bf16 q, which is exact only for powers of two"
    kern = functools.partial(_diff_kernel, tile=tile, scale=scale, lambda_init=lambda_init)
    return pl.pallas_call(
        kern,
        grid=(batch, DIFF_HEADS, n_tiles),
        in_specs=[pl.BlockSpec(lam_p.shape, lambda b, h, i: (0, 0)),
                  pl.BlockSpec((1, 1, d), lambda b, h, i: (h, 0, 0)),
                  pl.BlockSpec((tile, d), lambda b, h, i: (b * n_tiles + i, col0 + h)),
                  pl.BlockSpec((seq, d), lambda b, h, i: (b, col0 + DIFF_HEADS + h)),
                  pl.BlockSpec((seq, d), lambda b, h, i: (b, col0 + 2 * DIFF_HEADS + h))],
        out_specs=pl.BlockSpec((tile, d), lambda b, h, i: (b * n_tiles + i, h)),
        out_shape=jax.ShapeDtypeStruct((batch * seq, DIFF_WIDTH), BF16),
        scratch_shapes=[pltpu.VMEM((n_tiles, d, tile), BF16)],
        compiler_params=_params("parallel", "parallel", "arbitrary"),
        name="diff_attn",
    )(lam_p, gain[:, None, :], proj, proj, proj)


MERGE_TM = 512


def _merge_kernel(oa_ref, ob_ref, ga_ref, gb_ref, wa_ref, wb_ref, o_ref):
    a = jnp.dot(oa_ref[...], wa_ref[...], preferred_element_type=F32)
    b = jnp.dot(ob_ref[...], wb_ref[...], preferred_element_type=F32)
    o_ref[...] = (ga_ref[...].astype(F32) * a + gb_ref[...].astype(F32) * b).astype(o_ref.dtype)


def _merge(out_a, out_b, proj, wa, wb):
    t = out_a.shape[0]
    tm = min(MERGE_TM, t)
    gate_col = (3 * MOBA_WIDTH + 3 * DIFF_WIDTH) // D_MODEL
    const = lambda i: (0, 0)
    return pl.pallas_call(
        _merge_kernel,
        grid=(t // tm,),
        in_specs=[pl.BlockSpec((tm, MOBA_WIDTH), lambda i: (i, 0)),
                  pl.BlockSpec((tm, DIFF_WIDTH), lambda i: (i, 0)),
                  pl.BlockSpec((tm, D_MODEL), lambda i: (i, gate_col)),
                  pl.BlockSpec((tm, D_MODEL), lambda i: (i, gate_col + 1)),
                  pl.BlockSpec(wa.shape, const, pipeline_mode=pl.Buffered(1)),
                  pl.BlockSpec(wb.shape, const, pipeline_mode=pl.Buffered(1))],
        out_specs=pl.BlockSpec((tm, D_MODEL), lambda i: (i, 0)),
        out_shape=jax.ShapeDtypeStruct((t, D_MODEL), BF16),
        compiler_params=_params("parallel"),
        name="gated_merge",
    )(out_a, out_b, proj, proj, wa, wb)


def _layer_norm(z, gain, bias):
    mu = jnp.mean(z, axis=-1, keepdims=True)
    zc = z - mu
    var = jnp.mean(jnp.square(zc), axis=-1, keepdims=True)
    return zc * lax.rsqrt(var + LN_EPS) * gain + bias


def _out_ln_kernel(x_ref, m_ref, w_ref, g_ref, b_ref, o_ref, ot_ref):
    y = jnp.dot(m_ref[...], w_ref[...], preferred_element_type=F32)
    xn = _layer_norm(DEEPNORM_ALPHA * x_ref[...] + y, g_ref[...], b_ref[...])
    o_ref[...] = xn
    ot_ref[...] = xn.T.astype(ot_ref.dtype)


def _out_ln(x, merged, w, gain, bias):
    t, d = x.shape
    tm = min(MERGE_TM, t)
    const = lambda i: (0, 0)
    return pl.pallas_call(
        _out_ln_kernel,
        grid=(t // tm,),
        in_specs=[pl.BlockSpec((tm, d), lambda i: (i, 0)),
                  pl.BlockSpec((tm, d), lambda i: (i, 0)),
                  pl.BlockSpec(w.shape, const, pipeline_mode=pl.Buffered(1)),
                  pl.BlockSpec((1, d), const),
                  pl.BlockSpec((1, d), const)],
        out_specs=[pl.BlockSpec((tm, d), lambda i: (i, 0)),
                   pl.BlockSpec((d, tm), lambda i: (0, i))],
        out_shape=[jax.ShapeDtypeStruct((t, d), F32), jax.ShapeDtypeStruct((d, t), BF16)],
        compiler_params=_params("parallel"),
        name="out_proj_ln",
    )(x, merged, w, gain, bias)


ROUTE_TT = 256
GELU_C = math.sqrt(2.0 / math.pi)


def _top_rows(s, k):
    out = []
    for _ in range(k):
        m = jnp.max(s, axis=0, keepdims=True)
        out.append(m)
        s = jnp.where(s == m, -jnp.inf, s)
    return out


def _candidate_pool(top1, top2, k):
    neg = jnp.full((1, LANES), -jnp.inf, F32)

    def rows_of(vals, n):
        pad = (-n) % 8
        return jnp.concatenate(list(vals[:n]) + [neg] * pad, axis=0)

    slabs = []
    a = 0
    while a < k and k // (a + 1) > 1:
        slabs.append(top1[a] + rows_of(top2, k // (a + 1)))
        a += 1
    slabs.append(rows_of(top1[a:], k - a) + top2[0])
    return slabs


def _route_kernel(xt_ref, wq_ref, keys_ref, tau_ref, a1_ref, s2_ref, a2_ref, sc_ref):
    n_heads, n_keys, tt = s2_ref.shape
    half = keys_ref.shape[3]
    k = PEER_TOPK
    qt = jnp.dot(wq_ref[...], xt_ref[...], preferred_element_type=F32)
    for h in range(n_heads):
        for c in range(2):
            r0 = (2 * h + c) * half
            sc_ref[2 * h + c] = jnp.dot(keys_ref[c, h], qt[r0:r0 + half, :].astype(BF16),
                                        preferred_element_type=F32)

    def one(idx, _):
        h = idx // (tt // LANES)
        lanes = pl.ds(pl.multiple_of((idx % (tt // LANES)) * LANES, LANES), LANES)
        s1 = sc_ref[2 * h, :, lanes]
        s2 = sc_ref[2 * h + 1, :, lanes]
        top1 = _top_rows(s1, k + 1)
        top2 = _top_rows(s2, k + 1)
        pool = _candidate_pool(top1, top2, k + 1)
        m0 = top1[0] + top2[0]
        z = jnp.zeros_like(m0)
        best = []
        for r in range(k + 1):
            m = functools.reduce(jnp.maximum, [jnp.max(sl, axis=0, keepdims=True) for sl in pool])
            best.append(m)
            if r < k:
                z = z + jnp.exp(m - m0)
                pool = [jnp.where(sl == m, -jnp.inf, sl) for sl in pool]
        thr = 0.5 * (best[k - 1] + best[k])
        tau_ref[h, :, lanes] = thr - s1
        a1_ref[h, :, lanes] = 0.5 * jnp.exp(s1 - top1[0]) / z
        s2_ref[h, :, lanes] = s2
        a2_ref[h, :, lanes] = jnp.exp(s2 - top2[0])
        return 0

    lax.fori_loop(0, n_heads * (tt // LANES), one, 0)


def _route(xt, wq_t, subkeys):
    d, t = xt.shape
    tt = min(ROUTE_TT, t)
    big = pl.BlockSpec((PEER_HEADS, PEER_N_KEYS, tt), lambda i: (0, 0, i))
    big_shape = jax.ShapeDtypeStruct((PEER_HEADS, PEER_N_KEYS, t), F32)
    return pl.pallas_call(
        _route_kernel,
        grid=(t // tt,),
        in_specs=[pl.BlockSpec((d, tt), lambda i: (0, i)),
                  pl.BlockSpec(wq_t.shape, lambda i: (0, 0), pipeline_mode=pl.Buffered(1)),
                  pl.BlockSpec(subkeys.shape, lambda i: (0, 0, 0, 0))],
        out_specs=[big, big, big, big],
        out_shape=[big_shape, big_shape, big_shape, big_shape],
        scratch_shapes=[pltpu.VMEM((2 * PEER_HEADS, PEER_N_KEYS, tt), F32)],
        compiler_params=_params("parallel"),
        name="peer_route",
    )(xt, wq_t, subkeys)


PEER_TT = 512
PEER_EC = 1024
PEER_SUB = 256
PEER_PAIR = 2


def _peer_kernel(xt_ref, u_ref, vt_ref, tau_ref, a1_ref, s2_ref, a2_ref, o_ref, acc_ref, g_ref, *, n_chunks):
    c = pl.program_id(1)
    n_heads = s2_ref.shape[0]
    tt = xt_ref.shape[1]
    ec = u_ref.shape[0]

    @pl.when(c == 0)
    def _():
        acc_ref[...] = jnp.zeros_like(acc_ref)
        g_ref[1] = jnp.zeros(g_ref.shape[1:], g_ref.dtype)

    def second_matmul():
        return jnp.dot(vt_ref[...], g_ref[(c + 1) % 2], preferred_element_type=F32)

    def first_matmul_and_gate():
        xt = xt_ref[...]
        slot = c % 2
        for sub in range(ec // PEER_SUB):
            r0 = sub * PEER_SUB
            hid = jnp.dot(u_ref[r0:r0 + PEER_SUB, :], xt, preferred_element_type=F32)
            for rp in range(0, PEER_SUB // PEER_N_KEYS, PEER_PAIR):
                for lc in range(tt // LANES):
                    lanes = slice(lc * LANES, (lc + 1) * LANES)
                    i1s = [r0 // PEER_N_KEYS + rp + k for k in range(PEER_PAIR)]
                    ws = [jnp.zeros((PEER_N_KEYS, LANES), F32) for _ in i1s]
                    for h in range(n_heads):
                        s2 = s2_ref[h, :, lanes]
                        a2 = a2_ref[h, :, lanes]
                        for k, i1 in enumerate(i1s):
                            picked = jnp.where(s2 >= tau_ref[h, i1:i1 + 1, lanes], a2, 0.0)
                            ws[k] = ws[k] + a1_ref[h, i1:i1 + 1, lanes] * picked
                    for k, i1 in enumerate(i1s):
                        rows = slice((rp + k) * PEER_N_KEYS, (rp + k + 1) * PEER_N_KEYS)
                        x = hid[rows, lanes]
                        th = jnp.tanh(x * (x * x * (GELU_C * 0.044715) + GELU_C))
                        g = (x * th + x) * ws[k]
                        g_ref[slot, r0 + rows.start:r0 + rows.stop, lanes] = g.astype(g_ref.dtype)

    @pl.when(c < n_chunks)
    def _():
        acc_ref[...] += second_matmul()
        first_matmul_and_gate()

    @pl.when(c == n_chunks)
    def _():
        o_ref[...] = acc_ref[...] + second_matmul()


def _peer(xt, u_bf16, vt_bf16, tau, a1, s2, a2):
    d, t = xt.shape
    n_exp = u_bf16.shape[0]
    tt = min(PEER_TT, t)
    ec = PEER_EC
    n_chunks = n_exp // ec
    rows = ec // PEER_N_KEYS
    last = n_chunks - 1
    cur = lambda c: jnp.minimum(c, last)
    prev = lambda c: jnp.maximum(c - 1, 0)
    chunk = pl.BlockSpec((PEER_HEADS, rows, tt), lambda i, c: (0, cur(c), i))
    full = pl.BlockSpec((PEER_HEADS, PEER_N_KEYS, tt), lambda i, c: (0, 0, i))
    kern = functools.partial(_peer_kernel, n_chunks=n_chunks)
    return pl.pallas_call(
        kern,
        grid=(t // tt, n_chunks + 1),
        in_specs=[pl.BlockSpec((d, tt), lambda i, c: (0, i)),
                  pl.BlockSpec((ec, d), lambda i, c: (cur(c), 0)),
                  pl.BlockSpec((d, ec), lambda i, c: (0, prev(c))),
                  chunk, chunk, full, full],
        out_specs=pl.BlockSpec((d, tt), lambda i, c: (0, i)),
        out_shape=jax.ShapeDtypeStruct((d, t), F32),
        scratch_shapes=[pltpu.VMEM((d, tt), F32), pltpu.VMEM((2, ec, tt), BF16)],
        compiler_params=_params("parallel", "arbitrary"),
        name="peer_dense",
    )(xt, u_bf16, vt_bf16, tau, a1, s2, a2)


LN2_TM = 512


def _ln2_kernel(x_ref, yt_ref, g_ref, b_ref, o_ref):
    o_ref[...] = _layer_norm(DEEPNORM_ALPHA * x_ref[...] + yt_ref[...].T, g_ref[...], b_ref[...])


def _ln2(x, yt, gain, bias):
    t, d = x.shape
    tm = min(LN2_TM, t)
    const = lambda i: (0, 0)
    return pl.pallas_call(
        _ln2_kernel,
        grid=(t // tm,),
        in_specs=[pl.BlockSpec((tm, d), lambda i: (i, 0)),
                  pl.BlockSpec((d, tm), lambda i: (0, i)),
                  pl.BlockSpec((1, d), const),
                  pl.BlockSpec((1, d), const)],
        out_specs=pl.BlockSpec((tm, d), lambda i: (i, 0)),
        out_shape=jax.ShapeDtypeStruct((t, d), F32),
        compiler_params=_params("parallel"),
        name="residual_ln2",
    )(x, yt, gain, bias)


def _rope_tables(positions):
    pos = positions.reshape(-1).astype(F32)[:, None]

    def table(dim):
        inv_freq = ROPE_THETA ** (-jnp.arange(0, dim, 2, dtype=F32) / dim)
        ang = pos * inv_freq
        return jnp.cos(ang), jnp.sin(ang)

    ca, sa = table(MOBA_HEAD_DIM)
    cb, sb = table(DIFF_QK_DIM)
    return (jnp.concatenate([ca, ca], axis=1), jnp.concatenate([-sa, sa], axis=1),
            jnp.concatenate([cb, cb, cb, cb], axis=1), jnp.concatenate([-sb, sb, -sb, sb], axis=1))


def kernel(x, positions, w_in, w_branch_a, w_branch_b, w_out, diff_lambda, diff_norm_gain, ln1_gain, ln1_bias,
           ln2_gain, ln2_bias, peer_w_query, peer_subkeys, peer_u, peer_v):
    batch, seq, d = x.shape
    xf = x.reshape(batch * seq, d)
    tables = _rope_tables(positions)
    for layer in range(DEPTH):
        lambda_init = 0.8 - 0.6 * math.exp(-0.3 * layer)
        proj = _in_proj(xf, w_in[layer].astype(BF16), tables)
        out_a = _moba(proj, batch, seq)
        out_b = _diff(proj, diff_lambda[layer], diff_norm_gain[layer], batch, seq, lambda_init)
        merged = _merge(out_a, out_b, proj, w_branch_a[layer].astype(BF16), w_branch_b[layer].astype(BF16))
        x1, x1t = _out_ln(xf, merged, w_out[layer].astype(BF16), ln1_gain[layer][None], ln1_bias[layer][None])
        tau, a1, s2, a2 = _route(x1t, peer_w_query[layer].T.astype(BF16), peer_subkeys[layer].astype(BF16))
        yt = _peer(x1t, peer_u[layer].astype(BF16), peer_v[layer].T.astype(BF16), tau, a1, s2, a2)
        xf = _ln2(x1, yt, ln2_gain[layer][None], ln2_bias[layer][None])
    return xf.reshape(batch, seq, d)
```

```python
import functools
import math

import jax
import jax.numpy as jnp
from jax import lax
from jax.experimental import pallas as pl
from jax.experimental.pallas import tpu as pltpu

F32 = jnp.float32
BF16 = jnp.bfloat16

D_MODEL = 2048
DEPTH = 2
MOBA_HEADS = 8
MOBA_HEAD_DIM = 128
MOBA_BLOCK = 256
MOBA_TOPK = 3
MOBA_WIDTH = MOBA_HEADS * MOBA_HEAD_DIM
DIFF_HEADS = 8
DIFF_QK_DIM = 64
DIFF_V_DIM = 2 * DIFF_QK_DIM
DIFF_WIDTH = DIFF_HEADS * DIFF_V_DIM
IN_COLS = 3 * MOBA_WIDTH + 3 * DIFF_WIDTH + 2 * D_MODEL
PEER_HEADS = 8
PEER_N_KEYS = 128
PEER_N_EXPERTS = PEER_N_KEYS * PEER_N_KEYS
PEER_TOPK = 16
PEER_QUERY_DIM = 256
PEER_HALF = PEER_QUERY_DIM // 2
ROPE_THETA = 10000.0
LN_EPS = 1e-5
RMS_EPS = 1e-6
DEEPNORM_ALPHA = (2 * DEPTH) ** 0.25

LANES = 128
VMEM_LIMIT = 56 * 1024 * 1024
NEG_BIG = -1e30

NT_DIMS = (((1,), (1,)), ((), ()))


def _params(*sem):
    return pltpu.CompilerParams(dimension_semantics=sem, vmem_limit_bytes=VMEM_LIMIT)


PROJ_TM = 1024
PROJ_TN = 1024


def _rope_full(t, cos, sin_signed):
    return t * cos + pltpu.roll(t, LANES // 2, 1) * sin_signed


def _rope_pairs(t, cos, sin_signed):
    lane = lax.broadcasted_iota(jnp.int32, t.shape, 1)
    first_half = (lane % DIFF_QK_DIM) < (DIFF_QK_DIM // 2)
    partner = jnp.where(first_half, pltpu.roll(t, LANES - DIFF_QK_DIM // 2, 1), pltpu.roll(t, DIFF_QK_DIM // 2, 1))
    return t * cos + partner * sin_signed


def _in_proj_kernel(x_ref, w_ref, cosa_ref, sina_ref, cosb_ref, sinb_ref, o_ref, xb_ref):
    j = pl.program_id(1)

    @pl.when(j == 0)
    def _():
        xb_ref[...] = x_ref[...].astype(BF16)

    acc = jnp.dot(xb_ref[...], w_ref[...], preferred_element_type=F32)
    n_heads = acc.shape[1] // LANES
    moba_qk = j < 2
    diff_qk = (j == 3) | (j == 4)
    plain = (j == 2) | (j == 5)

    def store(fn):
        for h in range(n_heads):
            sl = slice(h * LANES, (h + 1) * LANES)
            o_ref[:, sl] = fn(acc[:, sl]).astype(o_ref.dtype)

    @pl.when(moba_qk)
    def _():
        store(lambda t: _rope_full(t, cosa_ref[...], sina_ref[...]))

    @pl.when(diff_qk)
    def _():
        store(lambda t: _rope_pairs(t, cosb_ref[...], sinb_ref[...]))

    @pl.when(plain)
    def _():
        store(lambda t: t)

    @pl.when(j >= 6)
    def _():
        store(jax.nn.sigmoid)


def _in_proj(x, w_bf16, tables):
    t, d = x.shape
    n = w_bf16.shape[1]
    tm = min(PROJ_TM, t)
    tab_spec = pl.BlockSpec((tm, LANES), lambda i, j: (i, 0))
    return pl.pallas_call(
        _in_proj_kernel,
        grid=(t // tm, n // PROJ_TN),
        in_specs=[pl.BlockSpec((tm, d), lambda i, j: (i, 0)),
                  pl.BlockSpec((d, PROJ_TN), lambda i, j: (0, j)),
                  tab_spec, tab_spec, tab_spec, tab_spec],
        out_specs=pl.BlockSpec((tm, PROJ_TN), lambda i, j: (i, j)),
        out_shape=jax.ShapeDtypeStruct((t, n), BF16),
        scratch_shapes=[pltpu.VMEM((tm, d), BF16)],
        compiler_params=_params("parallel", "arbitrary"),
        name="in_proj",
    )(x, w_bf16, *tables)


LOG2E = math.log2(math.e)


def _softmax_step(slabs, m_prev, l_prev, acc_prev, vt, c):
    m_cur = functools.reduce(jnp.maximum, [jnp.max(s, axis=0, keepdims=True) for s in slabs])
    m_new = jnp.maximum(m_prev, m_cur)
    alpha = jnp.exp2((m_prev - m_new) * c)
    ps = [jnp.exp2((s - m_new) * c) for s in slabs]
    l_new = alpha * l_prev + functools.reduce(jnp.add, [jnp.sum(p, axis=0, keepdims=True) for p in ps])
    p_all = ps[0] if len(ps) == 1 else jnp.concatenate(ps, axis=0)
    acc_new = alpha * acc_prev + jnp.dot(vt, p_all.astype(BF16), preferred_element_type=F32)
    return m_new, l_new, acc_new


def _softmax_init(d, n_queries):
    return (jnp.full((1, n_queries), NEG_BIG, F32), jnp.zeros((1, n_queries), F32), jnp.zeros((d, n_queries), F32))


def _causal(s):
    key_pos = lax.broadcasted_iota(jnp.int32, s.shape, 0)
    q_pos = lax.broadcasted_iota(jnp.int32, s.shape, 1)
    return jnp.where(key_pos <= q_pos, s, -jnp.inf)


def _transposed(v_ref, rows):
    return v_ref[rows, :].astype(F32).T.astype(BF16)


MOBA_GROUP = 4
MOBA_QBLOCKS = 2


def _moba_kernel(q_ref, k_ref, v_ref, o_ref, kmean_ref, vt_ref, sel_ref, *, blk, group, topk, c):
    i = pl.program_id(2)
    n_blocks = kmean_ref.shape[0]
    nq = MOBA_QBLOCKS
    d = q_ref.shape[1]
    tq = nq * blk

    @pl.when(i == 0)
    def _():
        for n in range(n_blocks):
            rows = slice(n * blk, (n + 1) * blk)
            kmean_ref[n:n + 1, :] = jnp.mean(k_ref[rows, :].astype(F32), axis=0, keepdims=True)
            vt_ref[n // group, :, (n % group) * blk:(n % group + 1) * blk] = _transposed(v_ref, rows)

    q = q_ref[...]
    first = i * nq
    gate = lax.dot_general(kmean_ref[...].astype(BF16), q, NT_DIMS, preferred_element_type=F32)
    rows = lax.broadcasted_iota(jnp.int32, gate.shape, 0)
    own = first + lax.broadcasted_iota(jnp.int32, gate.shape, 1) // blk
    past = rows < own
    g = jnp.where(past, gate, -jnp.inf)
    sel = jnp.zeros(gate.shape, F32)
    for _ in range(topk):
        m = jnp.max(g, axis=0, keepdims=True)
        idx = jnp.min(jnp.where(g == m, rows, n_blocks), axis=0, keepdims=True)
        pick = rows == idx
        sel = jnp.where(pick, 1.0, sel)
        g = jnp.where(pick, -jnp.inf, g)
    sel_ref[...] = jnp.where(past, sel, 0.0)

    def past_group(gi, carry):
        kg = k_ref[pl.ds(pl.multiple_of(gi * (group * blk), group * blk), group * blk), :]
        s = lax.dot_general(kg, q, NT_DIMS, preferred_element_type=F32)
        slabs = []
        for b in range(group):
            kb = gi * group + b
            picked = jnp.where(kb < first, sel_ref[pl.ds(kb, 1), :], 0.0) > 0.0
            slabs.append(jnp.where(picked, s[b * blk:(b + 1) * blk], -jnp.inf))
        return _softmax_step(slabs, *carry, vt_ref[gi], c)

    carry = lax.fori_loop(0, (first + group - 1) // group, past_group, _softmax_init(d, tq))

    k_own = k_ref[pl.ds(pl.multiple_of(first * blk, tq), tq), :]
    s = lax.dot_general(k_own, q, NT_DIMS, preferred_element_type=F32)
    q_pos = lax.broadcasted_iota(jnp.int32, (blk, tq), 1)
    slabs = []
    for b in range(nq):
        rs = slice(b * blk, (b + 1) * blk)
        key_pos = lax.broadcasted_iota(jnp.int32, (blk, tq), 0) + b * blk
        in_own = (q_pos >= b * blk) & (q_pos < (b + 1) * blk)
        picked = sel_ref[pl.ds(first + b, 1), :] > 0.0
        slabs.append(jnp.where(in_own, jnp.where(key_pos <= q_pos, s[rs], -jnp.inf),
                               jnp.where(picked, s[rs], -jnp.inf)))
    gq = first // group
    off = pl.multiple_of((first % group) * blk, tq)
    _, l, acc = _softmax_step(slabs, *carry, vt_ref[gq, :, pl.ds(off, tq)], c)
    o_ref[...] = (acc / l).T.astype(o_ref.dtype)


def _moba(proj, batch, seq):
    blk = MOBA_BLOCK
    d = MOBA_HEAD_DIM
    n_blocks = seq // blk
    nq = MOBA_QBLOCKS
    group = MOBA_GROUP
    assert n_blocks % group == 0 and group % nq == 0 and n_blocks % nq == 0
    kern = functools.partial(_moba_kernel, blk=blk, group=group, topk=min(MOBA_TOPK, n_blocks),
                             c=d ** -0.5 * LOG2E)
    n_steps = n_blocks // nq
    return pl.pallas_call(
        kern,
        grid=(batch, MOBA_HEADS, n_steps),
        in_specs=[pl.BlockSpec((nq * blk, d), lambda b, h, i: (b * n_steps + i, h)),
                  pl.BlockSpec((seq, d), lambda b, h, i: (b, MOBA_HEADS + h)),
                  pl.BlockSpec((seq, d), lambda b, h, i: (b, 2 * MOBA_HEADS + h))],
        out_specs=pl.BlockSpec((nq * blk, d), lambda b, h, i: (b * n_steps + i, h)),
        out_shape=jax.ShapeDtypeStruct((batch * seq, MOBA_WIDTH), BF16),
        scratch_shapes=[pltpu.VMEM((n_blocks, d), F32),
                        pltpu.VMEM((n_blocks // group, d, group * blk), BF16),
                        pltpu.VMEM((n_blocks, nq * blk), F32)],
        compiler_params=_params("parallel", "parallel", "arbitrary"),
        name="moba_attn",
    )(proj, proj, proj)


DIFF_TILE = 512


def _diff_kernel(lam_ref, gain_ref, q_ref, k_ref, v_ref, o_ref, vt_ref, vtp_ref, *, tile, scale, lambda_init):
    i = pl.program_id(2)
    d = v_ref.shape[1]

    @pl.when(i == 0)
    def _():
        for n in range(vt_ref.shape[0]):
            vt = _transposed(v_ref, slice(n * tile, (n + 1) * tile))
            vt_ref[n] = vt
            vtp_ref[n // 2, :, (n % 2) * tile:(n % 2 + 1) * tile] = vt

    q = q_ref[...] * scale
    lane = lax.broadcasted_iota(jnp.int32, q.shape, 1)
    zero = jnp.zeros_like(q)
    q1 = jnp.where(lane < DIFF_QK_DIM, q, zero)
    q2 = jnp.where(lane < DIFF_QK_DIM, zero, q)

    def scores(j, n_tiles=1):
        kj = k_ref[pl.ds(pl.multiple_of(j * tile, tile), n_tiles * tile), :]
        return (lax.dot_general(kj, q1, NT_DIMS, preferred_element_type=F32),
                lax.dot_general(kj, q2, NT_DIMS, preferred_element_type=F32))

    def past_pair(jp, carry):
        c1, c2 = carry
        s1, s2 = scores(2 * jp, 2)
        vt = vtp_ref[jp]
        return _softmax_step([s1], *c1, vt, LOG2E), _softmax_step([s2], *c2, vt, LOG2E)

    def past_tile(j, carry):
        c1, c2 = carry
        s1, s2 = scores(j)
        vt = vt_ref[j]
        return _softmax_step([s1], *c1, vt, LOG2E), _softmax_step([s2], *c2, vt, LOG2E)

    carry = lax.fori_loop(0, i // 2, past_pair, (_softmax_init(d, tile), _softmax_init(d, tile)))
    c1, c2 = lax.fori_loop(2 * (i // 2), i, past_tile, carry)

    s1, s2 = scores(i)
    vt = vt_ref[i]
    _, l1, a1 = _softmax_step([_causal(s1)], *c1, vt, LOG2E)
    _, l2, a2 = _softmax_step([_causal(s2)], *c2, vt, LOG2E)

    lp = lam_ref[...]
    lam = (jnp.exp(jnp.sum(lp[0:1] * lp[1:2], axis=1, keepdims=True))
           - jnp.exp(jnp.sum(lp[2:3] * lp[3:4], axis=1, keepdims=True)) + lambda_init)
    o = (a1 / l1 - lam * (a2 / l2)).T
    o = o * lax.rsqrt(jnp.mean(jnp.square(o), axis=1, keepdims=True) + RMS_EPS)
    o = o * gain_ref[0] * (1.0 - lambda_init)
    o_ref[...] = o.astype(o_ref.dtype)


def _diff(proj, lam_p, gain, batch, seq, lambda_init):
    tile = min(DIFF_TILE, seq)
    d = DIFF_V_DIM
    n_tiles = seq // tile
    col0 = 3 * MOBA_WIDTH // d
    scale = DIFF_QK_DIM ** -0.5
    assert math.frexp(scale)[0] == 0.5, "the score scale is folded into bf16 q, which is exact only for powers of two"
    kern = functools.partial(_diff_kernel, tile=tile, scale=scale, lambda_init=lambda_init)
    return pl.pallas_call(
        kern,
        grid=(batch, DIFF_HEADS, n_tiles),
        in_specs=[pl.BlockSpec(lam_p.shape, lambda b, h, i: (0, 0)),
                  pl.BlockSpec((1, 1, d), lambda b, h, i: (h, 0, 0)),
                  pl.BlockSpec((tile, d), lambda b, h, i: (b * n_tiles + i, col0 + h)),
                  pl.BlockSpec((seq, d), lambda b, h, i: (b, col0 + DIFF_HEADS + h)),
                  pl.BlockSpec((seq, d), lambda b, h, i: (b, col0 + 2 * DIFF_HEADS + h))],
        out_specs=pl.BlockSpec((tile, d), lambda b, h, i: (b * n_tiles + i, h)),
        out_shape=jax.ShapeDtypeStruct((batch * seq, DIFF_WIDTH), BF16),
        scratch_shapes=[pltpu.VMEM((n_tiles, d, tile), BF16), pltpu.VMEM((max(n_tiles // 2, 1), d, 2 * tile), BF16)],
        compiler_params=_params("parallel", "parallel", "arbitrary"),
        name="diff_attn",
    )(lam_p, gain[:, None, :], proj, proj, proj)


MERGE_TM = 512


def _merge_kernel(oa_ref, ob_ref, ga_ref, gb_ref, wa_ref, wb_ref, o_ref):
    a = jnp.dot(oa_ref[...], wa_ref[...], preferred_element_type=F32)
    b = jnp.dot(ob_ref[...], wb_ref[...], preferred_element_type=F32)
    o_ref[...] = (ga_ref[...].astype(F32) * a + gb_ref[...].astype(F32) * b).astype(o_ref.dtype)


def _merge(out_a, out_b, proj, wa, wb):
    t = out_a.shape[0]
    tm = min(MERGE_TM, t)
    gate_col = (3 * MOBA_WIDTH + 3 * DIFF_WIDTH) // D_MODEL
    const = lambda i: (0, 0)
    return pl.pallas_call(
        _merge_kernel,
        grid=(t // tm,),
        in_specs=[pl.BlockSpec((tm, MOBA_WIDTH), lambda i: (i, 0)),
                  pl.BlockSpec((tm, DIFF_WIDTH), lambda i: (i, 0)),
                  pl.BlockSpec((tm, D_MODEL), lambda i: (i, gate_col)),
                  pl.BlockSpec((tm, D_MODEL), lambda i: (i, gate_col + 1)),
                  pl.BlockSpec(wa.shape, const, pipeline_mode=pl.Buffered(1)),
                  pl.BlockSpec(wb.shape, const, pipeline_mode=pl.Buffered(1))],
        out_specs=pl.BlockSpec((tm, D_MODEL), lambda i: (i, 0)),
        out_shape=jax.ShapeDtypeStruct((t, D_MODEL), BF16),
        compiler_params=_params("parallel"),
        name="gated_merge",
    )(out_a, out_b, proj, proj, wa, wb)


def _layer_norm(z, gain, bias):
    mu = jnp.mean(z, axis=-1, keepdims=True)
    zc = z - mu
    var = jnp.mean(jnp.square(zc), axis=-1, keepdims=True)
    return zc * lax.rsqrt(var + LN_EPS) * gain + bias


def _out_ln_kernel(x_ref, m_ref, w_ref, g_ref, b_ref, o_ref, ot_ref):
    y = jnp.dot(m_ref[...], w_ref[...], preferred_element_type=F32)
    xn = _layer_norm(DEEPNORM_ALPHA * x_ref[...] + y, g_ref[...], b_ref[...])
    o_ref[...] = xn
    ot_ref[...] = xn.T.astype(ot_ref.dtype)


def _out_ln(x, merged, w, gain, bias):
    t, d = x.shape
    tm = min(MERGE_TM, t)
    const = lambda i: (0, 0)
    return pl.pallas_call(
        _out_ln_kernel,
        grid=(t // tm,),
        in_specs=[pl.BlockSpec((tm, d), lambda i: (i, 0)),
                  pl.BlockSpec((tm, d), lambda i: (i, 0)),
                  pl.BlockSpec(w.shape, const, pipeline_mode=pl.Buffered(1)),
                  pl.BlockSpec((1, d), const),
                  pl.BlockSpec((1, d), const)],
        out_specs=[pl.BlockSpec((tm, d), lambda i: (i, 0)),
                   pl.BlockSpec((d, tm), lambda i: (0, i))],
        out_shape=[jax.ShapeDtypeStruct((t, d), F32), jax.ShapeDtypeStruct((d, t), BF16)],
        compiler_params=_params("parallel"),
        name="out_proj_ln",
    )(x, merged, w, gain, bias)


ROUTE_TT = 256
GELU_C = math.sqrt(2.0 / math.pi)


def _sorting_network(n):
    pairs = []
    p = 1
    while p < n:
        k = p
        while k >= 1:
            for j in range(k % p, n - k, 2 * k):
                for i in range(min(k, n - j - k)):
                    if (i + j) // (2 * p) == (i + j + k) // (2 * p):
                        pairs.append((i + j, i + j + k))
            k //= 2
        p *= 2
    return pairs


def _top_rows(s, k):
    tiles = [s[r:r + 8] for r in range(0, s.shape[0], 8)]
    for lo, hi in _sorting_network(len(tiles)):
        tiles[lo], tiles[hi] = jnp.maximum(tiles[lo], tiles[hi]), jnp.minimum(tiles[lo], tiles[hi])
    out = []
    for r in range(k):
        m = jnp.max(tiles[0], axis=0, keepdims=True)
        out.append(m)
        depth = min(len(tiles), k - 1 - r)
        taken = tiles[0] == m
        tiles = [jnp.where(taken, tiles[j + 1] if j + 1 < len(tiles) else -jnp.inf, tiles[j]) for j in range(depth)]
    return out


def _candidate_pool(top1, top2, k):
    neg = jnp.full((1, LANES), -jnp.inf, F32)

    def rows_of(vals, n):
        pad = (-n) % 8
        return jnp.concatenate(list(vals[:n]) + [neg] * pad, axis=0)

    slabs = []
    a = 0
    while a < k and k // (a + 1) > 1:
        slabs.append(top1[a] + rows_of(top2, k // (a + 1)))
        a += 1
    slabs.append(rows_of(top1[a:], k - a) + top2[0])
    return slabs


def _route_kernel(xt_ref, wq_ref, keys_ref, tau_ref, a1_ref, s2_ref, a2_ref, sc_ref):
    n_heads, n_keys, tt = s2_ref.shape
    half = keys_ref.shape[3]
    k = PEER_TOPK
    qt = jnp.dot(wq_ref[...], xt_ref[...], preferred_element_type=F32)
    for h in range(n_heads):
        for c in range(2):
            r0 = (2 * h + c) * half
            sc_ref[2 * h + c] = jnp.dot(keys_ref[c, h], qt[r0:r0 + half, :].astype(BF16),
                                        preferred_element_type=F32)

    def one(h, lanes):
        s1 = sc_ref[2 * h, :, lanes]
        s2 = sc_ref[2 * h + 1, :, lanes]
        top1 = _top_rows(s1, k + 1)
        top2 = _top_rows(s2, k + 1)
        pool = _candidate_pool(top1, top2, k + 1)
        m0 = top1[0] + top2[0]
        z = jnp.zeros_like(m0)
        best = []
        for r in range(k + 1):
            m = functools.reduce(jnp.maximum, [jnp.max(sl, axis=0, keepdims=True) for sl in pool])
            best.append(m)
            if r < k:
                z = z + jnp.exp(m - m0)
                pool = [jnp.where(sl == m, -jnp.inf, sl) for sl in pool]
        thr = 0.5 * (best[k - 1] + best[k])
        tau_ref[h, :, lanes] = thr - s1
        a1_ref[h, :, lanes] = 0.5 * jnp.exp(s1 - top1[0]) / z
        s2_ref[h, :, lanes] = s2
        a2_ref[h, :, lanes] = jnp.exp(s2 - top2[0])

    def head(h, _):
        for lc in range(tt // LANES):
            one(h, slice(lc * LANES, (lc + 1) * LANES))
        return 0

    lax.fori_loop(0, n_heads, head, 0)


def _route(xt, wq_t, subkeys):
    d, t = xt.shape
    tt = min(ROUTE_TT, t)
    big = pl.BlockSpec((PEER_HEADS, PEER_N_KEYS, tt), lambda i: (0, 0, i))
    big_shape = jax.ShapeDtypeStruct((PEER_HEADS, PEER_N_KEYS, t), F32)
    return pl.pallas_call(
        _route_kernel,
        grid=(t // tt,),
        in_specs=[pl.BlockSpec((d, tt), lambda i: (0, i)),
                  pl.BlockSpec(wq_t.shape, lambda i: (0, 0), pipeline_mode=pl.Buffered(1)),
                  pl.BlockSpec(subkeys.shape, lambda i: (0, 0, 0, 0))],
        out_specs=[big, big, big, big],
        out_shape=[big_shape, big_shape, big_shape, big_shape],
        scratch_shapes=[pltpu.VMEM((2 * PEER_HEADS, PEER_N_KEYS, tt), F32)],
        compiler_params=_params("parallel"),
        name="peer_route",
    )(xt, wq_t, subkeys)


PEER_TT = 512
PEER_EC = 1024
PEER_SUB = 256
PEER_PAIR = 2


def _peer_kernel(xt_ref, u_ref, vt_ref, tau_ref, a1_ref, s2_ref, a2_ref, o_ref, acc_ref, g_ref, *, n_chunks):
    c = pl.program_id(1)
    n_heads = s2_ref.shape[0]
    tt = xt_ref.shape[1]
    ec = u_ref.shape[0]

    @pl.when(c == 0)
    def _():
        acc_ref[...] = jnp.zeros_like(acc_ref)
        g_ref[1] = jnp.zeros(g_ref.shape[1:], g_ref.dtype)

    def second_matmul():
        return jnp.dot(vt_ref[...], g_ref[(c + 1) % 2], preferred_element_type=F32)

    def first_matmul_and_gate():
        xt = xt_ref[...]
        slot = c % 2
        for sub in range(ec // PEER_SUB):
            r0 = sub * PEER_SUB
            hid = jnp.dot(u_ref[r0:r0 + PEER_SUB, :], xt, preferred_element_type=F32)
            for rp in range(0, PEER_SUB // PEER_N_KEYS, PEER_PAIR):
                for lc in range(tt // LANES):
                    lanes = slice(lc * LANES, (lc + 1) * LANES)
                    i1s = [r0 // PEER_N_KEYS + rp + k for k in range(PEER_PAIR)]
                    ws = [jnp.zeros((PEER_N_KEYS, LANES), F32) for _ in i1s]
                    for h in range(n_heads):
                        s2 = s2_ref[h, :, lanes]
                        a2 = a2_ref[h, :, lanes]
                        for k, i1 in enumerate(i1s):
                            picked = jnp.where(s2 >= tau_ref[h, i1:i1 + 1, lanes], a2, 0.0)
                            ws[k] = ws[k] + a1_ref[h, i1:i1 + 1, lanes] * picked
                    for k, i1 in enumerate(i1s):
                        rows = slice((rp + k) * PEER_N_KEYS, (rp + k + 1) * PEER_N_KEYS)
                        x = hid[rows, lanes]
                        th = jnp.tanh(x * (x * x * (GELU_C * 0.044715) + GELU_C))
                        g = (x * th + x) * ws[k]
                        g_ref[slot, r0 + rows.start:r0 + rows.stop, lanes] = g.astype(g_ref.dtype)

    @pl.when(c < n_chunks)
    def _():
        acc_ref[...] += second_matmul()
        first_matmul_and_gate()

    @pl.when(c == n_chunks)
    def _():
        o_ref[...] = acc_ref[...] + second_matmul()


def _peer(xt, u_bf16, vt_bf16, tau, a1, s2, a2):
    d, t = xt.shape
    n_exp = u_bf16.shape[0]
    tt = min(PEER_TT, t)
    ec = PEER_EC
    n_chunks = n_exp // ec
    rows = ec // PEER_N_KEYS
    last = n_chunks - 1
    cur = lambda c: jnp.minimum(c, last)
    prev = lambda c: jnp.maximum(c - 1, 0)
    chunk = pl.BlockSpec((PEER_HEADS, rows, tt), lambda i, c: (0, cur(c), i))
    full = pl.BlockSpec((PEER_HEADS, PEER_N_KEYS, tt), lambda i, c: (0, 0, i))
    kern = functools.partial(_peer_kernel, n_chunks=n_chunks)
    return pl.pallas_call(
        kern,
        grid=(t // tt, n_chunks + 1),
        in_specs=[pl.BlockSpec((d, tt), lambda i, c: (0, i)),
                  pl.BlockSpec((ec, d), lambda i, c: (cur(c), 0)),
                  pl.BlockSpec((d, ec), lambda i, c: (0, prev(c))),
                  chunk, chunk, full, full],
        out_specs=pl.BlockSpec((d, tt), lambda i, c: (0, i)),
        out_shape=jax.ShapeDtypeStruct((d, t), F32),
        scratch_shapes=[pltpu.VMEM((d, tt), F32), pltpu.VMEM((2, ec, tt), BF16)],
        compiler_params=_params("parallel", "arbitrary"),
        name="peer_dense",
    )(xt, u_bf16, vt_bf16, tau, a1, s2, a2)


LN2_TM = 512


def _ln2_kernel(x_ref, yt_ref, g_ref, b_ref, o_ref):
    o_ref[...] = _layer_norm(DEEPNORM_ALPHA * x_ref[...] + yt_ref[...].T, g_ref[...], b_ref[...])


def _ln2(x, yt, gain, bias):
    t, d = x.shape
    tm = min(LN2_TM, t)
    const = lambda i: (0, 0)
    return pl.pallas_call(
        _ln2_kernel,
        grid=(t // tm,),
        in_specs=[pl.BlockSpec((tm, d), lambda i: (i, 0)),
                  pl.BlockSpec((d, tm), lambda i: (0, i)),
                  pl.BlockSpec((1, d), const),
                  pl.BlockSpec((1, d), const)],
        out_specs=pl.BlockSpec((tm, d), lambda i: (i, 0)),
        out_shape=jax.ShapeDtypeStruct((t, d), F32),
        compiler_params=_params("parallel"),
        name="residual_ln2",
    )(x, yt, gain, bias)


def _rope_tables(positions):
    pos = positions.reshape(-1).astype(F32)[:, None]

    def table(dim):
        inv_freq = ROPE_THETA ** (-jnp.arange(0, dim, 2, dtype=F32) / dim)
        ang = pos * inv_freq
        return jnp.cos(ang), jnp.sin(ang)

    ca, sa = table(MOBA_HEAD_DIM)
    cb, sb = table(DIFF_QK_DIM)
    return (jnp.concatenate([ca, ca], axis=1), jnp.concatenate([-sa, sa], axis=1),
            jnp.concatenate([cb, cb, cb, cb], axis=1), jnp.concatenate([-sb, sb, -sb, sb], axis=1))


def kernel(x, positions, w_in, w_branch_a, w_branch_b, w_out, diff_lambda, diff_norm_gain, ln1_gain, ln1_bias,
           ln2_gain, ln2_bias, peer_w_query, peer_subkeys, peer_u, peer_v):
    batch, seq, d = x.shape
    xf = x.reshape(batch * seq, d)
    tables = _rope_tables(positions)
    for layer in range(DEPTH):
        lambda_init = 0.8 - 0.6 * math.exp(-0.3 * layer)
        proj = _in_proj(xf, w_in[layer].astype(BF16), tables)
        out_a = _moba(proj, batch, seq)
        out_b = _diff(proj, diff_lambda[layer], diff_norm_gain[layer], batch, seq, lambda_init)
        merged = _merge(out_a, out_b, proj, w_branch_a[layer].astype(BF16), w_branch_b[layer].astype(BF16))
        x1, x1t = _out_ln(xf, merged, w_out[layer].astype(BF16), ln1_gain[layer][None], ln1_bias[layer][None])
        tau, a1, s2, a2 = _route(x1t, peer_w_query[layer].T.astype(BF16), peer_subkeys[layer].astype(BF16))
        yt = _peer(x1t, peer_u[layer].astype(BF16), peer_v[layer].T.astype(BF16), tau, a1, s2, a2)
        xf = _ln2(x1, yt, ln2_gain[layer][None], ln2_bias[layer][None])
    return xf.reshape(batch, seq, d)
```

```python
import functools
import math

import jax
import jax.numpy as jnp
from jax import lax
from jax.experimental import pallas as pl
from jax.experimental.pallas import tpu as pltpu

F32 = jnp.float32
BF16 = jnp.bfloat16

D_MODEL = 2048
DEPTH = 2
MOBA_HEADS = 8
MOBA_HEAD_DIM = 128
MOBA_BLOCK = 256
MOBA_TOPK = 3
MOBA_WIDTH = MOBA_HEADS * MOBA_HEAD_DIM
DIFF_HEADS = 8
DIFF_QK_DIM = 64
DIFF_V_DIM = 2 * DIFF_QK_DIM
DIFF_WIDTH = DIFF_HEADS * DIFF_V_DIM
IN_COLS = 3 * MOBA_WIDTH + 3 * DIFF_WIDTH + 2 * D_MODEL
PEER_HEADS = 8
PEER_N_KEYS = 128
PEER_N_EXPERTS = PEER_N_KEYS * PEER_N_KEYS
PEER_TOPK = 16
PEER_QUERY_DIM = 256
PEER_HALF = PEER_QUERY_DIM // 2
ROPE_THETA = 10000.0
LN_EPS = 1e-5
RMS_EPS = 1e-6
DEEPNORM_ALPHA = (2 * DEPTH) ** 0.25

LANES = 128
VMEM_LIMIT = 56 * 1024 * 1024
NEG_BIG = -1e30

NT_DIMS = (((1,), (1,)), ((), ()))


def _params(*sem):
    return pltpu.CompilerParams(dimension_semantics=sem, vmem_limit_bytes=VMEM_LIMIT)


PROJ_TM = 1024
PROJ_TN = 1024


def _rope_full(t, cos, sin_signed):
    return t * cos + pltpu.roll(t, LANES // 2, 1) * sin_signed


def _rope_pairs(t, cos, sin_signed):
    lane = lax.broadcasted_iota(jnp.int32, t.shape, 1)
    first_half = (lane % DIFF_QK_DIM) < (DIFF_QK_DIM // 2)
    partner = jnp.where(first_half, pltpu.roll(t, LANES - DIFF_QK_DIM // 2, 1), pltpu.roll(t, DIFF_QK_DIM // 2, 1))
    return t * cos + partner * sin_signed


def _in_proj_kernel(x_ref, w_ref, cosa_ref, sina_ref, cosb_ref, sinb_ref, o_ref, xb_ref):
    j = pl.program_id(1)

    @pl.when(j == 0)
    def _():
        xb_ref[...] = x_ref[...].astype(BF16)

    acc = jnp.dot(xb_ref[...], w_ref[...], preferred_element_type=F32)
    n_heads = acc.shape[1] // LANES
    moba_qk = j < 2
    diff_qk = (j == 3) | (j == 4)
    plain = (j == 2) | (j == 5)

    def store(fn):
        for h in range(n_heads):
            sl = slice(h * LANES, (h + 1) * LANES)
            o_ref[:, sl] = fn(acc[:, sl]).astype(o_ref.dtype)

    @pl.when(moba_qk)
    def _():
        store(lambda t: _rope_full(t, cosa_ref[...], sina_ref[...]))

    @pl.when(diff_qk)
    def _():
        store(lambda t: _rope_pairs(t, cosb_ref[...], sinb_ref[...]))

    @pl.when(plain)
    def _():
        store(lambda t: t)

    @pl.when(j >= 6)
    def _():
        store(jax.nn.sigmoid)


def _in_proj(x, w_bf16, tables):
    t, d = x.shape
    n = w_bf16.shape[1]
    tm = min(PROJ_TM, t)
    tab_spec = pl.BlockSpec((tm, LANES), lambda i, j: (i, 0))
    return pl.pallas_call(
        _in_proj_kernel,
        grid=(t // tm, n // PROJ_TN),
        in_specs=[pl.BlockSpec((tm, d), lambda i, j: (i, 0)),
                  pl.BlockSpec((d, PROJ_TN), lambda i, j: (0, j)),
                  tab_spec, tab_spec, tab_spec, tab_spec],
        out_specs=pl.BlockSpec((tm, PROJ_TN), lambda i, j: (i, j)),
        out_shape=jax.ShapeDtypeStruct((t, n), BF16),
        scratch_shapes=[pltpu.VMEM((tm, d), BF16)],
        compiler_params=_params("parallel", "arbitrary"),
        name="in_proj",
    )(x, w_bf16, *tables)


LOG2E = math.log2(math.e)


def _softmax_step(slabs, m_prev, l_prev, acc_prev, vt, c):
    m_cur = functools.reduce(jnp.maximum, [jnp.max(s, axis=0, keepdims=True) for s in slabs])
    m_new = jnp.maximum(m_prev, m_cur)
    alpha = jnp.exp2((m_prev - m_new) * c)
    ps = [jnp.exp2((s - m_new) * c) for s in slabs]
    l_new = alpha * l_prev + functools.reduce(jnp.add, [jnp.sum(p, axis=0, keepdims=True) for p in ps])
    p_all = ps[0] if len(ps) == 1 else jnp.concatenate(ps, axis=0)
    acc_new = alpha * acc_prev + jnp.dot(vt, p_all.astype(BF16), preferred_element_type=F32)
    return m_new, l_new, acc_new


def _softmax_init(d, n_queries):
    return (jnp.full((1, n_queries), NEG_BIG, F32), jnp.zeros((1, n_queries), F32), jnp.zeros((d, n_queries), F32))


def _causal(s):
    key_pos = lax.broadcasted_iota(jnp.int32, s.shape, 0)
    q_pos = lax.broadcasted_iota(jnp.int32, s.shape, 1)
    return jnp.where(key_pos <= q_pos, s, -jnp.inf)


def _transposed(v_ref, rows):
    return v_ref[rows, :].astype(F32).T.astype(BF16)


MOBA_GROUP = 4
MOBA_QBLOCKS = 2
ATTN_HEADS = 2


def _head(ref, hh, rows=slice(None)):
    return ref[rows, hh * LANES:(hh + 1) * LANES]


def _moba_kernel(q_ref, k_ref, v_ref, o_ref, kmean_ref, vt_ref, sel_ref, *, blk, group, topk, c):
    i = pl.program_id(2)
    nh, n_blocks = kmean_ref.shape[:2]
    nq = MOBA_QBLOCKS
    d = LANES
    tq = nq * blk
    first = i * nq

    @pl.when(i == 0)
    def _():
        for hh in range(nh):
            for n in range(n_blocks):
                rows = slice(n * blk, (n + 1) * blk)
                kmean_ref[hh, n:n + 1, :] = jnp.mean(_head(k_ref, hh, rows).astype(F32), axis=0, keepdims=True)
                vt_ref[hh, n // group, :, (n % group) * blk:(n % group + 1) * blk] = (
                    _head(v_ref, hh, rows).astype(F32).T.astype(BF16))

    qs = [_head(q_ref, hh) for hh in range(nh)]
    for hh in range(nh):
        gate = lax.dot_general(kmean_ref[hh].astype(BF16), qs[hh], NT_DIMS, preferred_element_type=F32)
        rows = lax.broadcasted_iota(jnp.int32, gate.shape, 0)
        own = first + lax.broadcasted_iota(jnp.int32, gate.shape, 1) // blk
        past = rows < own
        g = jnp.where(past, gate, -jnp.inf)
        sel = jnp.zeros(gate.shape, F32)
        for _ in range(topk):
            m = jnp.max(g, axis=0, keepdims=True)
            idx = jnp.min(jnp.where(g == m, rows, n_blocks), axis=0, keepdims=True)
            pick = rows == idx
            sel = jnp.where(pick, 1.0, sel)
            g = jnp.where(pick, -jnp.inf, g)
        sel_ref[hh] = jnp.where(past, sel, 0.0)

    def past_group(gi, carries):
        out = []
        for hh in range(nh):
            kg = _head(k_ref, hh, pl.ds(pl.multiple_of(gi * (group * blk), group * blk), group * blk))
            s = lax.dot_general(kg, qs[hh], NT_DIMS, preferred_element_type=F32)
            slabs = []
            for b in range(group):
                kb = gi * group + b
                picked = jnp.where(kb < first, sel_ref[hh, pl.ds(kb, 1), :], 0.0) > 0.0
                slabs.append(jnp.where(picked, s[b * blk:(b + 1) * blk], -jnp.inf))
            out.append(_softmax_step(slabs, *carries[hh], vt_ref[hh, gi], c))
        return tuple(out)

    carries = lax.fori_loop(0, (first + group - 1) // group, past_group,
                            tuple(_softmax_init(d, tq) for _ in range(nh)))

    q_pos = lax.broadcasted_iota(jnp.int32, (blk, tq), 1)
    gq = first // group
    off = pl.multiple_of((first % group) * blk, tq)
    for hh in range(nh):
        k_own = _head(k_ref, hh, pl.ds(pl.multiple_of(first * blk, tq), tq))
        s = lax.dot_general(k_own, qs[hh], NT_DIMS, preferred_element_type=F32)
        slabs = []
        for b in range(nq):
            rs = slice(b * blk, (b + 1) * blk)
            key_pos = lax.broadcasted_iota(jnp.int32, (blk, tq), 0) + b * blk
            in_own = (q_pos >= b * blk) & (q_pos < (b + 1) * blk)
            picked = sel_ref[hh, pl.ds(first + b, 1), :] > 0.0
            slabs.append(jnp.where(in_own, jnp.where(key_pos <= q_pos, s[rs], -jnp.inf),
                                   jnp.where(picked, s[rs], -jnp.inf)))
        _, l, acc = _softmax_step(slabs, *carries[hh], vt_ref[hh, gq, :, pl.ds(off, tq)], c)
        o_ref[:, hh * LANES:(hh + 1) * LANES] = (acc / l).T.astype(o_ref.dtype)


def _moba(proj, batch, seq):
    blk = MOBA_BLOCK
    d = MOBA_HEAD_DIM
    n_blocks = seq // blk
    nq = MOBA_QBLOCKS
    nh = ATTN_HEADS
    group = MOBA_GROUP
    assert d == LANES and n_blocks % group == 0 and group % nq == 0 and n_blocks % nq == 0 and MOBA_HEADS % nh == 0
    kern = functools.partial(_moba_kernel, blk=blk, group=group, topk=min(MOBA_TOPK, n_blocks),
                             c=d ** -0.5 * LOG2E)
    n_steps = n_blocks // nq
    hg = MOBA_HEADS // nh
    return pl.pallas_call(
        kern,
        grid=(batch, hg, n_steps),
        in_specs=[pl.BlockSpec((nq * blk, nh * d), lambda b, h, i: (b * n_steps + i, h)),
                  pl.BlockSpec((seq, nh * d), lambda b, h, i: (b, hg + h)),
                  pl.BlockSpec((seq, nh * d), lambda b, h, i: (b, 2 * hg + h))],
        out_specs=pl.BlockSpec((nq * blk, nh * d), lambda b, h, i: (b * n_steps + i, h)),
        out_shape=jax.ShapeDtypeStruct((batch * seq, MOBA_WIDTH), BF16),
        scratch_shapes=[pltpu.VMEM((nh, n_blocks, d), F32),
                        pltpu.VMEM((nh, n_blocks // group, d, group * blk), BF16),
                        pltpu.VMEM((nh, n_blocks, nq * blk), F32)],
        compiler_params=_params("parallel", "parallel", "arbitrary"),
        name="moba_attn",
    )(proj, proj, proj)


DIFF_TILE = 512


def _diff_kernel(lam_ref, gain_ref, q_ref, k_ref, v_ref, o_ref, vt_ref, vtp_ref, *, tile, scale, lambda_init):
    i = pl.program_id(2)
    nh = vt_ref.shape[0]
    d = LANES

    @pl.when(i == 0)
    def _():
        for hh in range(nh):
            for n in range(vt_ref.shape[1]):
                vt = _head(v_ref, hh, slice(n * tile, (n + 1) * tile)).astype(F32).T.astype(BF16)
                vt_ref[hh, n] = vt
                vtp_ref[hh, n // 2, :, (n % 2) * tile:(n % 2 + 1) * tile] = vt

    lane = lax.broadcasted_iota(jnp.int32, (tile, LANES), 1)
    q1, q2 = [], []
    for hh in range(nh):
        q = _head(q_ref, hh) * scale
        zero = jnp.zeros_like(q)
        q1.append(jnp.where(lane < DIFF_QK_DIM, q, zero))
        q2.append(jnp.where(lane < DIFF_QK_DIM, zero, q))

    def scores(hh, j, n_tiles=1):
        kj = _head(k_ref, hh, pl.ds(pl.multiple_of(j * tile, tile), n_tiles * tile))
        return (lax.dot_general(kj, q1[hh], NT_DIMS, preferred_element_type=F32),
                lax.dot_general(kj, q2[hh], NT_DIMS, preferred_element_type=F32))

    def step(hh, carry, s12, vt, mask=lambda s: s):
        c1, c2 = carry
        return (_softmax_step([mask(s12[0])], *c1, vt, LOG2E), _softmax_step([mask(s12[1])], *c2, vt, LOG2E))

    def past_pair(jp, carries):
        return tuple(step(hh, carries[hh], scores(hh, 2 * jp, 2), vtp_ref[hh, jp]) for hh in range(nh))

    def past_tile(j, carries):
        return tuple(step(hh, carries[hh], scores(hh, j), vt_ref[hh, j]) for hh in range(nh))

    init = tuple((_softmax_init(d, tile), _softmax_init(d, tile)) for _ in range(nh))
    carries = lax.fori_loop(0, i // 2, past_pair, init)
    carries = lax.fori_loop(2 * (i // 2), i, past_tile, carries)

    lp = lam_ref[...]
    lam = (jnp.exp(jnp.sum(lp[0:1] * lp[1:2], axis=1, keepdims=True))
           - jnp.exp(jnp.sum(lp[2:3] * lp[3:4], axis=1, keepdims=True)) + lambda_init)
    for hh in range(nh):
        (_, l1, a1), (_, l2, a2) = step(hh, carries[hh], scores(hh, i), vt_ref[hh, i], _causal)
        o = (a1 / l1 - lam * (a2 / l2)).T
        o = o * lax.rsqrt(jnp.mean(jnp.square(o), axis=1, keepdims=True) + RMS_EPS)
        o = o * gain_ref[hh] * (1.0 - lambda_init)
        o_ref[:, hh * LANES:(hh + 1) * LANES] = o.astype(o_ref.dtype)


def _diff(proj, lam_p, gain, batch, seq, lambda_init):
    tile = min(DIFF_TILE, seq)
    d = DIFF_V_DIM
    nh = ATTN_HEADS
    n_tiles = seq // tile
    hg = DIFF_HEADS // nh
    col0 = 3 * MOBA_WIDTH // (nh * d)
    scale = DIFF_QK_DIM ** -0.5
    assert math.frexp(scale)[0] == 0.5, "the score scale is folded into bf16 q, which is exact only for powers of two"
    assert d == LANES and DIFF_HEADS % nh == 0 and (3 * MOBA_WIDTH) % (nh * d) == 0
    kern = functools.partial(_diff_kernel, tile=tile, scale=scale, lambda_init=lambda_init)
    return pl.pallas_call(
        kern,
        grid=(batch, hg, n_tiles),
        in_specs=[pl.BlockSpec(lam_p.shape, lambda b, h, i: (0, 0)),
                  pl.BlockSpec((nh, 1, d), lambda b, h, i: (h, 0, 0)),
                  pl.BlockSpec((tile, nh * d), lambda b, h, i: (b * n_tiles + i, col0 + h)),
                  pl.BlockSpec((seq, nh * d), lambda b, h, i: (b, col0 + hg + h)),
                  pl.BlockSpec((seq, nh * d), lambda b, h, i: (b, col0 + 2 * hg + h))],
        out_specs=pl.BlockSpec((tile, nh * d), lambda b, h, i: (b * n_tiles + i, h)),
        out_shape=jax.ShapeDtypeStruct((batch * seq, DIFF_WIDTH), BF16),
        scratch_shapes=[pltpu.VMEM((nh, n_tiles, d, tile), BF16),
                        pltpu.VMEM((nh, max(n_tiles // 2, 1), d, 2 * tile), BF16)],
        compiler_params=_params("parallel", "parallel", "arbitrary"),
        name="diff_attn",
    )(lam_p, gain[:, None, :], proj, proj, proj)


MERGE_TM = 512


def _merge_kernel(oa_ref, ob_ref, ga_ref, gb_ref, wa_ref, wb_ref, o_ref):
    a = jnp.dot(oa_ref[...], wa_ref[...], preferred_element_type=F32)
    b = jnp.dot(ob_ref[...], wb_ref[...], preferred_element_type=F32)
    o_ref[...] = (ga_ref[...].astype(F32) * a + gb_ref[...].astype(F32) * b).astype(o_ref.dtype)


def _merge(out_a, out_b, proj, wa, wb):
    t = out_a.shape[0]
    tm = min(MERGE_TM, t)
    gate_col = (3 * MOBA_WIDTH + 3 * DIFF_WIDTH) // D_MODEL
    const = lambda i: (0, 0)
    return pl.pallas_call(
        _merge_kernel,
        grid=(t // tm,),
        in_specs=[pl.BlockSpec((tm, MOBA_WIDTH), lambda i: (i, 0)),
                  pl.BlockSpec((tm, DIFF_WIDTH), lambda i: (i, 0)),
                  pl.BlockSpec((tm, D_MODEL), lambda i: (i, gate_col)),
                  pl.BlockSpec((tm, D_MODEL), lambda i: (i, gate_col + 1)),
                  pl.BlockSpec(wa.shape, const, pipeline_mode=pl.Buffered(1)),
                  pl.BlockSpec(wb.shape, const, pipeline_mode=pl.Buffered(1))],
        out_specs=pl.BlockSpec((tm, D_MODEL), lambda i: (i, 0)),
        out_shape=jax.ShapeDtypeStruct((t, D_MODEL), BF16),
        compiler_params=_params("parallel"),
        name="gated_merge",
    )(out_a, out_b, proj, proj, wa, wb)


def _layer_norm(z, gain, bias):
    mu = jnp.mean(z, axis=-1, keepdims=True)
    zc = z - mu
    var = jnp.mean(jnp.square(zc), axis=-1, keepdims=True)
    return zc * lax.rsqrt(var + LN_EPS) * gain + bias


def _out_ln_kernel(x_ref, m_ref, w_ref, g_ref, b_ref, o_ref, ot_ref):
    y = jnp.dot(m_ref[...], w_ref[...], preferred_element_type=F32)
    xn = _layer_norm(DEEPNORM_ALPHA * x_ref[...] + y, g_ref[...], b_ref[...])
    o_ref[...] = xn
    ot_ref[...] = xn.T.astype(ot_ref.dtype)


def _out_ln(x, merged, w, gain, bias):
    t, d = x.shape
    tm = min(MERGE_TM, t)
    const = lambda i: (0, 0)
    return pl.pallas_call(
        _out_ln_kernel,
        grid=(t // tm,),
        in_specs=[pl.BlockSpec((tm, d), lambda i: (i, 0)),
                  pl.BlockSpec((tm, d), lambda i: (i, 0)),
                  pl.BlockSpec(w.shape, const, pipeline_mode=pl.Buffered(1)),
                  pl.BlockSpec((1, d), const),
                  pl.BlockSpec((1, d), const)],
        out_specs=[pl.BlockSpec((tm, d), lambda i: (i, 0)),
                   pl.BlockSpec((d, tm), lambda i: (0, i))],
        out_shape=[jax.ShapeDtypeStruct((t, d), F32), jax.ShapeDtypeStruct((d, t), BF16)],
        compiler_params=_params("parallel"),
        name="out_proj_ln",
    )(x, merged, w, gain, bias)


ROUTE_TT = 256
GELU_C = math.sqrt(2.0 / math.pi)


def _sorting_network(n):
    pairs = []
    p = 1
    while p < n:
        k = p
        while k >= 1:
            for j in range(k % p, n - k, 2 * k):
                for i in range(min(k, n - j - k)):
                    if (i + j) // (2 * p) == (i + j + k) // (2 * p):
                        pairs.append((i + j, i + j + k))
            k //= 2
        p *= 2
    return pairs


def _top_rows(s, k):
    tiles = [s[r:r + 8] for r in range(0, s.shape[0], 8)]
    for lo, hi in _sorting_network(len(tiles)):
        tiles[lo], tiles[hi] = jnp.maximum(tiles[lo], tiles[hi]), jnp.minimum(tiles[lo], tiles[hi])
    out = []
    for r in range(k):
        m = jnp.max(tiles[0], axis=0, keepdims=True)
        out.append(m)
        depth = min(len(tiles), k - 1 - r)
        taken = tiles[0] == m
        tiles = [jnp.where(taken, tiles[j + 1] if j + 1 < len(tiles) else -jnp.inf, tiles[j]) for j in range(depth)]
    return out


def _candidate_pool(top1, top2, k):
    neg = jnp.full((1, LANES), -jnp.inf, F32)

    def rows_of(vals, n):
        pad = (-n) % 8
        return jnp.concatenate(list(vals[:n]) + [neg] * pad, axis=0)

    slabs = []
    a = 0
    while a < k and k // (a + 1) > 1:
        slabs.append(top1[a] + rows_of(top2, k // (a + 1)))
        a += 1
    slabs.append(rows_of(top1[a:], k - a) + top2[0])
    return slabs


def _route_kernel(xt_ref, wq_ref, keys_ref, tau_ref, a1_ref, s2_ref, a2_ref, sc_ref):
    n_heads, n_keys, tt = s2_ref.shape
    half = keys_ref.shape[3]
    k = PEER_TOPK
    qt = jnp.dot(wq_ref[...], xt_ref[...], preferred_element_type=F32)
    for h in range(n_heads):
        for c in range(2):
            r0 = (2 * h + c) * half
            sc_ref[2 * h + c] = jnp.dot(keys_ref[c, h], qt[r0:r0 + half, :].astype(BF16),
                                        preferred_element_type=F32)

    def one(h, lanes):
        s1 = sc_ref[2 * h, :, lanes]
        s2 = sc_ref[2 * h + 1, :, lanes]
        top1 = _top_rows(s1, k + 1)
        top2 = _top_rows(s2, k + 1)
        pool = _candidate_pool(top1, top2, k + 1)
        m0 = top1[0] + top2[0]
        z = jnp.zeros_like(m0)
        best = []
        for r in range(k + 1):
            m = functools.reduce(jnp.maximum, [jnp.max(sl, axis=0, keepdims=True) for sl in pool])
            best.append(m)
            if r < k:
                z = z + jnp.exp(m - m0)
                pool = [jnp.where(sl == m, -jnp.inf, sl) for sl in pool]
        thr = 0.5 * (best[k - 1] + best[k])
        tau_ref[h, :, lanes] = thr - s1
        a1_ref[h, :, lanes] = 0.5 * jnp.exp(s1 - top1[0]) / z
        s2_ref[h, :, lanes] = s2
        a2_ref[h, :, lanes] = jnp.exp(s2 - top2[0])

    def head(h, _):
        for lc in range(tt // LANES):
            one(h, slice(lc * LANES, (lc + 1) * LANES))
        return 0

    lax.fori_loop(0, n_heads, head, 0)


def _route(xt, wq_t, subkeys):
    d, t = xt.shape
    tt = min(ROUTE_TT, t)
    big = pl.BlockSpec((PEER_HEADS, PEER_N_KEYS, tt), lambda i: (0, 0, i))
    big_shape = jax.ShapeDtypeStruct((PEER_HEADS, PEER_N_KEYS, t), F32)
    return pl.pallas_call(
        _route_kernel,
        grid=(t // tt,),
        in_specs=[pl.BlockSpec((d, tt), lambda i: (0, i)),
                  pl.BlockSpec(wq_t.shape, lambda i: (0, 0), pipeline_mode=pl.Buffered(1)),
                  pl.BlockSpec(subkeys.shape, lambda i: (0, 0, 0, 0))],
        out_specs=[big, big, big, big],
        out_shape=[big_shape, big_shape, big_shape, big_shape],
        scratch_shapes=[pltpu.VMEM((2 * PEER_HEADS, PEER_N_KEYS, tt), F32)],
        compiler_params=_params("parallel"),
        name="peer_route",
    )(xt, wq_t, subkeys)


PEER_TT = 512
PEER_EC = 1024
PEER_SUB = 128
PEER_PAIR = 1


def _peer_kernel(xt_ref, u_ref, vt_ref, tau_ref, a1_ref, s2_ref, a2_ref, x_ref, gain_ref, bias_ref, o_ref,
                 acc_ref, g_ref, *, n_chunks):
    c = pl.program_id(1)
    n_heads = s2_ref.shape[0]
    tt = xt_ref.shape[1]
    ec = u_ref.shape[0]

    @pl.when(c == 0)
    def _():
        acc_ref[...] = jnp.zeros_like(acc_ref)
        g_ref[1] = jnp.zeros(g_ref.shape[1:], g_ref.dtype)

    def second_matmul():
        return jnp.dot(vt_ref[...], g_ref[(c + 1) % 2], preferred_element_type=F32)

    def first_matmul_and_gate():
        xt = xt_ref[...]
        slot = c % 2
        for sub in range(ec // PEER_SUB):
            r0 = sub * PEER_SUB
            hid = jnp.dot(u_ref[r0:r0 + PEER_SUB, :], xt, preferred_element_type=F32)
            for rp in range(0, PEER_SUB // PEER_N_KEYS, PEER_PAIR):
                for lc in range(tt // LANES):
                    lanes = slice(lc * LANES, (lc + 1) * LANES)
                    i1s = [r0 // PEER_N_KEYS + rp + k for k in range(PEER_PAIR)]
                    ws = [jnp.zeros((PEER_N_KEYS, LANES), F32) for _ in i1s]
                    for h in range(n_heads):
                        s2 = s2_ref[h, :, lanes]
                        a2 = a2_ref[h, :, lanes]
                        for k, i1 in enumerate(i1s):
                            picked = jnp.where(s2 >= tau_ref[h, i1:i1 + 1, lanes], a2, 0.0)
                            ws[k] = ws[k] + a1_ref[h, i1:i1 + 1, lanes] * picked
                    for k, i1 in enumerate(i1s):
                        rows = slice((rp + k) * PEER_N_KEYS, (rp + k + 1) * PEER_N_KEYS)
                        x = hid[rows, lanes]
                        th = jnp.tanh(x * (x * x * (GELU_C * 0.044715) + GELU_C))
                        g = (x * th + x) * ws[k]
                        g_ref[slot, r0 + rows.start:r0 + rows.stop, lanes] = g.astype(g_ref.dtype)

    @pl.when(c < n_chunks)
    def _():
        acc_ref[...] += second_matmul()
        first_matmul_and_gate()

    @pl.when(c == n_chunks)
    def _():
        y = (acc_ref[...] + second_matmul()).T
        o_ref[...] = _layer_norm(DEEPNORM_ALPHA * x_ref[...] + y, gain_ref[...], bias_ref[...])


def _peer(xt, u_bf16, vt_bf16, tau, a1, s2, a2, x, gain, bias):
    d, t = xt.shape
    n_exp = u_bf16.shape[0]
    tt = min(PEER_TT, t)
    ec = PEER_EC
    n_chunks = n_exp // ec
    rows = ec // PEER_N_KEYS
    last = n_chunks - 1
    cur = lambda c: jnp.minimum(c, last)
    prev = lambda c: jnp.maximum(c - 1, 0)
    chunk = pl.BlockSpec((PEER_HEADS, rows, tt), lambda i, c: (0, cur(c), i))
    full = pl.BlockSpec((PEER_HEADS, PEER_N_KEYS, tt), lambda i, c: (0, 0, i))
    kern = functools.partial(_peer_kernel, n_chunks=n_chunks)
    return pl.pallas_call(
        kern,
        grid=(t // tt, n_chunks + 1),
        in_specs=[pl.BlockSpec((d, tt), lambda i, c: (0, i)),
                  pl.BlockSpec((ec, d), lambda i, c: (cur(c), 0)),
                  pl.BlockSpec((d, ec), lambda i, c: (0, prev(c))),
                  chunk, chunk, full, full,
                  pl.BlockSpec((tt, d), lambda i, c: (i, 0), pipeline_mode=pl.Buffered(1)),
                  pl.BlockSpec((1, d), lambda i, c: (0, 0)),
                  pl.BlockSpec((1, d), lambda i, c: (0, 0))],
        out_specs=pl.BlockSpec((tt, d), lambda i, c: (i, 0)),
        out_shape=jax.ShapeDtypeStruct((t, d), F32),
        scratch_shapes=[pltpu.VMEM((d, tt), F32), pltpu.VMEM((2, ec, tt), BF16)],
        compiler_params=_params("parallel", "arbitrary"),
        name="peer_dense",
    )(xt, u_bf16, vt_bf16, tau, a1, s2, a2, x, gain, bias)


def _rope_tables(positions):
    pos = positions.reshape(-1).astype(F32)[:, None]

    def table(dim):
        inv_freq = ROPE_THETA ** (-jnp.arange(0, dim, 2, dtype=F32) / dim)
        ang = pos * inv_freq
        return jnp.cos(ang), jnp.sin(ang)

    ca, sa = table(MOBA_HEAD_DIM)
    cb, sb = table(DIFF_QK_DIM)
    return (jnp.concatenate([ca, ca], axis=1), jnp.concatenate([-sa, sa], axis=1),
            jnp.concatenate([cb, cb, cb, cb], axis=1), jnp.concatenate([-sb, sb, -sb, sb], axis=1))


def kernel(x, positions, w_in, w_branch_a, w_branch_b, w_out, diff_lambda, diff_norm_gain, ln1_gain, ln1_bias,
           ln2_gain, ln2_bias, peer_w_query, peer_subkeys, peer_u, peer_v):
    batch, seq, d = x.shape
    xf = x.reshape(batch * seq, d)
    tables = _rope_tables(positions)
    for layer in range(DEPTH):
        lambda_init = 0.8 - 0.6 * math.exp(-0.3 * layer)
        proj = _in_proj(xf, w_in[layer].astype(BF16), tables)
        out_a = _moba(proj, batch, seq)
        out_b = _diff(proj, diff_lambda[layer], diff_norm_gain[layer], batch, seq, lambda_init)
        merged = _merge(out_a, out_b, proj, w_branch_a[layer].astype(BF16), w_branch_b[layer].astype(BF16))
        x1, x1t = _out_ln(xf, merged, w_out[layer].astype(BF16), ln1_gain[layer][None], ln1_bias[layer][None])
        tau, a1, s2, a2 = _route(x1t, peer_w_query[layer].T.astype(BF16), peer_subkeys[layer].astype(BF16))
        xf = _peer(x1t, peer_u[layer].astype(BF16), peer_v[layer].T.astype(BF16), tau, a1, s2, a2,
                   x1, ln2_gain[layer][None], ln2_bias[layer][None])
    return xf.reshape(batch, seq, d)
```

```python
import functools
import math

import jax
import jax.numpy as jnp
from jax import lax
from jax.experimental import pallas as pl
from jax.experimental.pallas import tpu as pltpu

F32 = jnp.float32
BF16 = jnp.bfloat16

D_MODEL = 2048
DEPTH = 2
MOBA_HEADS = 8
MOBA_HEAD_DIM = 128
MOBA_BLOCK = 256
MOBA_TOPK = 3
MOBA_WIDTH = MOBA_HEADS * MOBA_HEAD_DIM
DIFF_HEADS = 8
DIFF_QK_DIM = 64
DIFF_V_DIM = 2 * DIFF_QK_DIM
DIFF_WIDTH = DIFF_HEADS * DIFF_V_DIM
IN_COLS = 3 * MOBA_WIDTH + 3 * DIFF_WIDTH + 2 * D_MODEL
PEER_HEADS = 8
PEER_N_KEYS = 128
PEER_N_EXPERTS = PEER_N_KEYS * PEER_N_KEYS
PEER_TOPK = 16
PEER_QUERY_DIM = 256
PEER_HALF = PEER_QUERY_DIM // 2
ROPE_THETA = 10000.0
LN_EPS = 1e-5
RMS_EPS = 1e-6
DEEPNORM_ALPHA = (2 * DEPTH) ** 0.25

LANES = 128
VMEM_LIMIT = 56 * 1024 * 1024
NEG_BIG = -1e30

NT_DIMS = (((1,), (1,)), ((), ()))


def _params(*sem):
    return pltpu.CompilerParams(dimension_semantics=sem, vmem_limit_bytes=VMEM_LIMIT)


PROJ_TM = 1024
PROJ_TN = 1024


def _rope_full(t, cos, sin_signed):
    return t * cos + pltpu.roll(t, LANES // 2, 1) * sin_signed


def _rope_pairs(t, cos, sin_signed):
    lane = lax.broadcasted_iota(jnp.int32, t.shape, 1)
    first_half = (lane % DIFF_QK_DIM) < (DIFF_QK_DIM // 2)
    partner = jnp.where(first_half, pltpu.roll(t, LANES - DIFF_QK_DIM // 2, 1), pltpu.roll(t, DIFF_QK_DIM // 2, 1))
    return t * cos + partner * sin_signed


def _in_proj_kernel(x_ref, w_ref, cosa_ref, sina_ref, cosb_ref, sinb_ref, o_ref, xb_ref):
    j = pl.program_id(1)

    @pl.when(j == 0)
    def _():
        xb_ref[...] = x_ref[...].astype(BF16)

    acc = jnp.dot(xb_ref[...], w_ref[...], preferred_element_type=F32)
    n_heads = acc.shape[1] // LANES
    moba_qk = j < 2
    diff_qk = (j == 3) | (j == 4)
    plain = (j == 2) | (j == 5)

    def store(fn):
        for h in range(n_heads):
            sl = slice(h * LANES, (h + 1) * LANES)
            o_ref[:, sl] = fn(acc[:, sl]).astype(o_ref.dtype)

    @pl.when(moba_qk)
    def _():
        store(lambda t: _rope_full(t, cosa_ref[...], sina_ref[...]))

    @pl.when(diff_qk)
    def _():
        store(lambda t: _rope_pairs(t, cosb_ref[...], sinb_ref[...]))

    @pl.when(plain)
    def _():
        store(lambda t: t)

    @pl.when(j >= 6)
    def _():
        store(jax.nn.sigmoid)


def _in_proj(x, w_bf16, tables):
    t, d = x.shape
    n = w_bf16.shape[1]
    tm = min(PROJ_TM, t)
    tab_spec = pl.BlockSpec((tm, LANES), lambda i, j: (i, 0))
    return pl.pallas_call(
        _in_proj_kernel,
        grid=(t // tm, n // PROJ_TN),
        in_specs=[pl.BlockSpec((tm, d), lambda i, j: (i, 0)),
                  pl.BlockSpec((d, PROJ_TN), lambda i, j: (0, j)),
                  tab_spec, tab_spec, tab_spec, tab_spec],
        out_specs=pl.BlockSpec((tm, PROJ_TN), lambda i, j: (i, j)),
        out_shape=jax.ShapeDtypeStruct((t, n), BF16),
        scratch_shapes=[pltpu.VMEM((tm, d), BF16)],
        compiler_params=_params("parallel", "arbitrary"),
        name="in_proj",
    )(x, w_bf16, *tables)


LOG2E = math.log2(math.e)


def _softmax_step(slabs, m_prev, l_prev, acc_prev, vt, c):
    m_cur = functools.reduce(jnp.maximum, [jnp.max(s, axis=0, keepdims=True) for s in slabs])
    m_new = jnp.maximum(m_prev, m_cur)
    alpha = jnp.exp2((m_prev - m_new) * c)
    ps = [jnp.exp2((s - m_new) * c) for s in slabs]
    l_new = alpha * l_prev + functools.reduce(jnp.add, [jnp.sum(p, axis=0, keepdims=True) for p in ps])
    p_all = ps[0] if len(ps) == 1 else jnp.concatenate(ps, axis=0)
    acc_new = alpha * acc_prev + jnp.dot(vt, p_all.astype(BF16), preferred_element_type=F32)
    return m_new, l_new, acc_new


def _softmax_init(d, n_queries):
    return (jnp.full((1, n_queries), NEG_BIG, F32), jnp.zeros((1, n_queries), F32), jnp.zeros((d, n_queries), F32))


def _causal(s):
    key_pos = lax.broadcasted_iota(jnp.int32, s.shape, 0)
    q_pos = lax.broadcasted_iota(jnp.int32, s.shape, 1)
    return jnp.where(key_pos <= q_pos, s, -jnp.inf)


def _transposed(v_ref, rows):
    return v_ref[rows, :].astype(F32).T.astype(BF16)


MOBA_GROUP = 4
MOBA_QBLOCKS = 4
ATTN_HEADS = 4


def _head(ref, hh, rows=slice(None)):
    return ref[rows, hh * LANES:(hh + 1) * LANES]


def _moba_kernel(q_ref, k_ref, v_ref, o_ref, kmean_ref, vt_ref, sel_ref, *, blk, group, topk, c):
    i = pl.program_id(2)
    nh, n_blocks = kmean_ref.shape[:2]
    nq = MOBA_QBLOCKS
    d = LANES
    tq = nq * blk
    first = i * nq

    @pl.when(i == 0)
    def _():
        for hh in range(nh):
            for n in range(n_blocks):
                rows = slice(n * blk, (n + 1) * blk)
                kmean_ref[hh, n:n + 1, :] = jnp.mean(_head(k_ref, hh, rows).astype(F32), axis=0, keepdims=True)
                vt_ref[hh, n // group, :, (n % group) * blk:(n % group + 1) * blk] = (
                    _head(v_ref, hh, rows).astype(F32).T.astype(BF16))

    qs = [_head(q_ref, hh) for hh in range(nh)]
    for hh in range(nh):
        gate = lax.dot_general(kmean_ref[hh].astype(BF16), qs[hh], NT_DIMS, preferred_element_type=F32)
        rows = lax.broadcasted_iota(jnp.int32, gate.shape, 0)
        own = first + lax.broadcasted_iota(jnp.int32, gate.shape, 1) // blk
        past = rows < own
        g = jnp.where(past, gate, -jnp.inf)
        sel = jnp.zeros(gate.shape, F32)
        for _ in range(topk):
            m = jnp.max(g, axis=0, keepdims=True)
            idx = jnp.min(jnp.where(g == m, rows, n_blocks), axis=0, keepdims=True)
            pick = rows == idx
            sel = jnp.where(pick, 1.0, sel)
            g = jnp.where(pick, -jnp.inf, g)
        sel_ref[hh] = jnp.where(past, sel, 0.0)

    def past_group(gi, carries):
        out = []
        for hh in range(nh):
            kg = _head(k_ref, hh, pl.ds(pl.multiple_of(gi * (group * blk), group * blk), group * blk))
            s = lax.dot_general(kg, qs[hh], NT_DIMS, preferred_element_type=F32)
            slabs = []
            for b in range(group):
                kb = gi * group + b
                picked = jnp.where(kb < first, sel_ref[hh, pl.ds(kb, 1), :], 0.0) > 0.0
                slabs.append(jnp.where(picked, s[b * blk:(b + 1) * blk], -jnp.inf))
            out.append(_softmax_step(slabs, *carries[hh], vt_ref[hh, gi], c))
        return tuple(out)

    carries = lax.fori_loop(0, (first + group - 1) // group, past_group,
                            tuple(_softmax_init(d, tq) for _ in range(nh)))

    q_pos = lax.broadcasted_iota(jnp.int32, (blk, tq), 1)
    gq = first // group
    off = pl.multiple_of((first % group) * blk, tq)
    for hh in range(nh):
        k_own = _head(k_ref, hh, pl.ds(pl.multiple_of(first * blk, tq), tq))
        s = lax.dot_general(k_own, qs[hh], NT_DIMS, preferred_element_type=F32)
        slabs = []
        for b in range(nq):
            rs = slice(b * blk, (b + 1) * blk)
            key_pos = lax.broadcasted_iota(jnp.int32, (blk, tq), 0) + b * blk
            in_own = (q_pos >= b * blk) & (q_pos < (b + 1) * blk)
            picked = sel_ref[hh, pl.ds(first + b, 1), :] > 0.0
            slabs.append(jnp.where(in_own, jnp.where(key_pos <= q_pos, s[rs], -jnp.inf),
                                   jnp.where(picked, s[rs], -jnp.inf)))
        _, l, acc = _softmax_step(slabs, *carries[hh], vt_ref[hh, gq, :, pl.ds(off, tq)], c)
        o_ref[:, hh * LANES:(hh + 1) * LANES] = (acc / l).T.astype(o_ref.dtype)


def _moba(proj, batch, seq):
    blk = MOBA_BLOCK
    d = MOBA_HEAD_DIM
    n_blocks = seq // blk
    nq = MOBA_QBLOCKS
    nh = ATTN_HEADS
    group = MOBA_GROUP
    assert d == LANES and n_blocks % group == 0 and group % nq == 0 and n_blocks % nq == 0 and MOBA_HEADS % nh == 0
    kern = functools.partial(_moba_kernel, blk=blk, group=group, topk=min(MOBA_TOPK, n_blocks),
                             c=d ** -0.5 * LOG2E)
    n_steps = n_blocks // nq
    hg = MOBA_HEADS // nh
    return pl.pallas_call(
        kern,
        grid=(batch, hg, n_steps),
        in_specs=[pl.BlockSpec((nq * blk, nh * d), lambda b, h, i: (b * n_steps + i, h)),
                  pl.BlockSpec((seq, nh * d), lambda b, h, i: (b, hg + h)),
                  pl.BlockSpec((seq, nh * d), lambda b, h, i: (b, 2 * hg + h))],
        out_specs=pl.BlockSpec((nq * blk, nh * d), lambda b, h, i: (b * n_steps + i, h)),
        out_shape=jax.ShapeDtypeStruct((batch * seq, MOBA_WIDTH), BF16),
        scratch_shapes=[pltpu.VMEM((nh, n_blocks, d), F32),
                        pltpu.VMEM((nh, n_blocks // group, d, group * blk), BF16),
                        pltpu.VMEM((nh, n_blocks, nq * blk), F32)],
        compiler_params=_params("parallel", "parallel", "arbitrary"),
        name="moba_attn",
    )(proj, proj, proj)


DIFF_TILE = 512


def _diff_kernel(lam_ref, gain_ref, q_ref, k_ref, v_ref, o_ref, vt_ref, vtp_ref, *, tile, scale, lambda_init):
    i = pl.program_id(2)
    nh = vt_ref.shape[0]
    d = LANES

    @pl.when(i == 0)
    def _():
        for hh in range(nh):
            for n in range(vt_ref.shape[1]):
                vt = _head(v_ref, hh, slice(n * tile, (n + 1) * tile)).astype(F32).T.astype(BF16)
                vt_ref[hh, n] = vt
                vtp_ref[hh, n // 2, :, (n % 2) * tile:(n % 2 + 1) * tile] = vt

    lane = lax.broadcasted_iota(jnp.int32, (tile, LANES), 1)
    q1, q2 = [], []
    for hh in range(nh):
        q = _head(q_ref, hh) * scale
        zero = jnp.zeros_like(q)
        q1.append(jnp.where(lane < DIFF_QK_DIM, q, zero))
        q2.append(jnp.where(lane < DIFF_QK_DIM, zero, q))

    def scores(hh, j, n_tiles=1):
        kj = _head(k_ref, hh, pl.ds(pl.multiple_of(j * tile, tile), n_tiles * tile))
        return (lax.dot_general(kj, q1[hh], NT_DIMS, preferred_element_type=F32),
                lax.dot_general(kj, q2[hh], NT_DIMS, preferred_element_type=F32))

    def step(hh, carry, s12, vt, mask=lambda s: s):
        c1, c2 = carry
        return (_softmax_step([mask(s12[0])], *c1, vt, LOG2E), _softmax_step([mask(s12[1])], *c2, vt, LOG2E))

    def past_pair(jp, carries):
        return tuple(step(hh, carries[hh], scores(hh, 2 * jp, 2), vtp_ref[hh, jp]) for hh in range(nh))

    def past_tile(j, carries):
        return tuple(step(hh, carries[hh], scores(hh, j), vt_ref[hh, j]) for hh in range(nh))

    init = tuple((_softmax_init(d, tile), _softmax_init(d, tile)) for _ in range(nh))
    carries = lax.fori_loop(0, i // 2, past_pair, init)
    carries = lax.fori_loop(2 * (i // 2), i, past_tile, carries)

    lp = lam_ref[...]
    lam = (jnp.exp(jnp.sum(lp[0:1] * lp[1:2], axis=1, keepdims=True))
           - jnp.exp(jnp.sum(lp[2:3] * lp[3:4], axis=1, keepdims=True)) + lambda_init)
    for hh in range(nh):
        (_, l1, a1), (_, l2, a2) = step(hh, carries[hh], scores(hh, i), vt_ref[hh, i], _causal)
        o = (a1 / l1 - lam * (a2 / l2)).T
        o = o * lax.rsqrt(jnp.mean(jnp.square(o), axis=1, keepdims=True) + RMS_EPS)
        o = o * gain_ref[hh] * (1.0 - lambda_init)
        o_ref[:, hh * LANES:(hh + 1) * LANES] = o.astype(o_ref.dtype)


def _diff(proj, lam_p, gain, batch, seq, lambda_init):
    tile = min(DIFF_TILE, seq)
    d = DIFF_V_DIM
    nh = ATTN_HEADS
    n_tiles = seq // tile
    hg = DIFF_HEADS // nh
    col0 = 3 * MOBA_WIDTH // (nh * d)
    scale = DIFF_QK_DIM ** -0.5
    assert math.frexp(scale)[0] == 0.5, "the score scale is folded into bf16 q, which is exact only for powers of two"
    assert d == LANES and DIFF_HEADS % nh == 0 and (3 * MOBA_WIDTH) % (nh * d) == 0
    kern = functools.partial(_diff_kernel, tile=tile, scale=scale, lambda_init=lambda_init)
    return pl.pallas_call(
        kern,
        grid=(batch, hg, n_tiles),
        in_specs=[pl.BlockSpec(lam_p.shape, lambda b, h, i: (0, 0)),
                  pl.BlockSpec((nh, 1, d), lambda b, h, i: (h, 0, 0)),
                  pl.BlockSpec((tile, nh * d), lambda b, h, i: (b * n_tiles + i, col0 + h)),
                  pl.BlockSpec((seq, nh * d), lambda b, h, i: (b, col0 + hg + h)),
                  pl.BlockSpec((seq, nh * d), lambda b, h, i: (b, col0 + 2 * hg + h))],
        out_specs=pl.BlockSpec((tile, nh * d), lambda b, h, i: (b * n_tiles + i, h)),
        out_shape=jax.ShapeDtypeStruct((batch * seq, DIFF_WIDTH), BF16),
        scratch_shapes=[pltpu.VMEM((nh, n_tiles, d, tile), BF16),
                        pltpu.VMEM((nh, max(n_tiles // 2, 1), d, 2 * tile), BF16)],
        compiler_params=_params("parallel", "parallel", "arbitrary"),
        name="diff_attn",
    )(lam_p, gain[:, None, :], proj, proj, proj)


MERGE_TM = 512


def _merge_kernel(oa_ref, ob_ref, ga_ref, gb_ref, wa_ref, wb_ref, o_ref):
    a = jnp.dot(oa_ref[...], wa_ref[...], preferred_element_type=F32)
    b = jnp.dot(ob_ref[...], wb_ref[...], preferred_element_type=F32)
    o_ref[...] = (ga_ref[...].astype(F32) * a + gb_ref[...].astype(F32) * b).astype(o_ref.dtype)


def _merge(out_a, out_b, proj, wa, wb):
    t = out_a.shape[0]
    tm = min(MERGE_TM, t)
    gate_col = (3 * MOBA_WIDTH + 3 * DIFF_WIDTH) // D_MODEL
    const = lambda i: (0, 0)
    return pl.pallas_call(
        _merge_kernel,
        grid=(t // tm,),
        in_specs=[pl.BlockSpec((tm, MOBA_WIDTH), lambda i: (i, 0)),
                  pl.BlockSpec((tm, DIFF_WIDTH), lambda i: (i, 0)),
                  pl.BlockSpec((tm, D_MODEL), lambda i: (i, gate_col)),
                  pl.BlockSpec((tm, D_MODEL), lambda i: (i, gate_col + 1)),
                  pl.BlockSpec(wa.shape, const, pipeline_mode=pl.Buffered(1)),
                  pl.BlockSpec(wb.shape, const, pipeline_mode=pl.Buffered(1))],
        out_specs=pl.BlockSpec((tm, D_MODEL), lambda i: (i, 0)),
        out_shape=jax.ShapeDtypeStruct((t, D_MODEL), BF16),
        compiler_params=_params("parallel"),
        name="gated_merge",
    )(out_a, out_b, proj, proj, wa, wb)


def _layer_norm(z, gain, bias):
    mu = jnp.mean(z, axis=-1, keepdims=True)
    zc = z - mu
    var = jnp.mean(jnp.square(zc), axis=-1, keepdims=True)
    return zc * lax.rsqrt(var + LN_EPS) * gain + bias


def _out_ln_kernel(x_ref, m_ref, w_ref, g_ref, b_ref, o_ref, ot_ref):
    y = jnp.dot(m_ref[...], w_ref[...], preferred_element_type=F32)
    xn = _layer_norm(DEEPNORM_ALPHA * x_ref[...] + y, g_ref[...], b_ref[...])
    o_ref[...] = xn
    ot_ref[...] = xn.T.astype(ot_ref.dtype)


def _out_ln(x, merged, w, gain, bias):
    t, d = x.shape
    tm = min(MERGE_TM, t)
    const = lambda i: (0, 0)
    return pl.pallas_call(
        _out_ln_kernel,
        grid=(t // tm,),
        in_specs=[pl.BlockSpec((tm, d), lambda i: (i, 0)),
                  pl.BlockSpec((tm, d), lambda i: (i, 0)),
                  pl.BlockSpec(w.shape, const, pipeline_mode=pl.Buffered(1)),
                  pl.BlockSpec((1, d), const),
                  pl.BlockSpec((1, d), const)],
        out_specs=[pl.BlockSpec((tm, d), lambda i: (i, 0)),
                   pl.BlockSpec((d, tm), lambda i: (0, i))],
        out_shape=[jax.ShapeDtypeStruct((t, d), F32), jax.ShapeDtypeStruct((d, t), BF16)],
        compiler_params=_params("parallel"),
        name="out_proj_ln",
    )(x, merged, w, gain, bias)


ROUTE_TT = 256
GELU_C = math.sqrt(2.0 / math.pi)


def _sorting_network(n):
    pairs = []
    p = 1
    while p < n:
        k = p
        while k >= 1:
            for j in range(k % p, n - k, 2 * k):
                for i in range(min(k, n - j - k)):
                    if (i + j) // (2 * p) == (i + j + k) // (2 * p):
                        pairs.append((i + j, i + j + k))
            k //= 2
        p *= 2
    return pairs


def _top_rows(s, k):
    tiles = [s[r:r + 8] for r in range(0, s.shape[0], 8)]
    for lo, hi in _sorting_network(len(tiles)):
        tiles[lo], tiles[hi] = jnp.maximum(tiles[lo], tiles[hi]), jnp.minimum(tiles[lo], tiles[hi])
    out = []
    for r in range(k):
        m = jnp.max(tiles[0], axis=0, keepdims=True)
        out.append(m)
        depth = min(len(tiles), k - 1 - r)
        taken = tiles[0] == m
        tiles = [jnp.where(taken, tiles[j + 1] if j + 1 < len(tiles) else -jnp.inf, tiles[j]) for j in range(depth)]
    return out


def _candidate_pool(top1, top2, k):
    neg = jnp.full((1, LANES), -jnp.inf, F32)

    def rows_of(vals, n):
        pad = (-n) % 8
        return jnp.concatenate(list(vals[:n]) + [neg] * pad, axis=0)

    slabs = []
    a = 0
    while a < k and k // (a + 1) > 1:
        slabs.append(top1[a] + rows_of(top2, k // (a + 1)))
        a += 1
    slabs.append(rows_of(top1[a:], k - a) + top2[0])
    return slabs


def _route_kernel(xt_ref, wq_ref, keys_ref, tau_ref, a1_ref, s2_ref, a2_ref, sc_ref):
    n_heads, n_keys, tt = s2_ref.shape
    half = keys_ref.shape[3]
    k = PEER_TOPK
    qt = jnp.dot(wq_ref[...], xt_ref[...], preferred_element_type=F32)
    for h in range(n_heads):
        for c in range(2):
            r0 = (2 * h + c) * half
            sc_ref[2 * h + c] = jnp.dot(keys_ref[c, h], qt[r0:r0 + half, :].astype(BF16),
                                        preferred_element_type=F32)

    def one(h, lanes):
        s1 = sc_ref[2 * h, :, lanes]
        s2 = sc_ref[2 * h + 1, :, lanes]
        top1 = _top_rows(s1, k + 1)
        top2 = _top_rows(s2, k + 1)
        pool = _candidate_pool(top1, top2, k + 1)
        m0 = top1[0] + top2[0]
        z = jnp.zeros_like(m0)
        best = []
        for r in range(k + 1):
            m = functools.reduce(jnp.maximum, [jnp.max(sl, axis=0, keepdims=True) for sl in pool])
            best.append(m)
            if r < k:
                z = z + jnp.exp(m - m0)
                pool = [jnp.where(sl == m, -jnp.inf, sl) for sl in pool]
        thr = 0.5 * (best[k - 1] + best[k])
        tau_ref[h, :, lanes] = thr - s1
        a1_ref[h, :, lanes] = 0.5 * jnp.exp(s1 - top1[0]) / z
        s2_ref[h, :, lanes] = s2
        a2_ref[h, :, lanes] = jnp.exp(s2 - top2[0])

    def head(h, _):
        for lc in range(tt // LANES):
            one(h, slice(lc * LANES, (lc + 1) * LANES))
        return 0

    lax.fori_loop(0, n_heads, head, 0)


def _route(xt, wq_t, subkeys):
    d, t = xt.shape
    tt = min(ROUTE_TT, t)
    big = pl.BlockSpec((PEER_HEADS, PEER_N_KEYS, tt), lambda i: (0, 0, i))
    big_shape = jax.ShapeDtypeStruct((PEER_HEADS, PEER_N_KEYS, t), F32)
    return pl.pallas_call(
        _route_kernel,
        grid=(t // tt,),
        in_specs=[pl.BlockSpec((d, tt), lambda i: (0, i)),
                  pl.BlockSpec(wq_t.shape, lambda i: (0, 0), pipeline_mode=pl.Buffered(1)),
                  pl.BlockSpec(subkeys.shape, lambda i: (0, 0, 0, 0))],
        out_specs=[big, big, big, big],
        out_shape=[big_shape, big_shape, big_shape, big_shape],
        scratch_shapes=[pltpu.VMEM((2 * PEER_HEADS, PEER_N_KEYS, tt), F32)],
        compiler_params=_params("parallel"),
        name="peer_route",
    )(xt, wq_t, subkeys)


PEER_TT = 512
PEER_EC = 1024
PEER_SUB = 256
PEER_PAIR = 2


def _peer_kernel(xt_ref, u_ref, vt_ref, tau_ref, a1_ref, s2_ref, a2_ref, x_ref, gain_ref, bias_ref, o_ref,
                 acc_ref, g_ref, *, n_chunks):
    c = pl.program_id(1)
    n_heads = s2_ref.shape[0]
    tt = xt_ref.shape[1]
    ec = u_ref.shape[0]

    @pl.when(c == 0)
    def _():
        acc_ref[...] = jnp.zeros_like(acc_ref)
        g_ref[1] = jnp.zeros(g_ref.shape[1:], g_ref.dtype)

    def second_matmul():
        return jnp.dot(vt_ref[...], g_ref[(c + 1) % 2], preferred_element_type=F32)

    def first_matmul_and_gate():
        xt = xt_ref[...]
        slot = c % 2
        for sub in range(ec // PEER_SUB):
            r0 = sub * PEER_SUB
            hid = jnp.dot(u_ref[r0:r0 + PEER_SUB, :], xt, preferred_element_type=F32)
            for rp in range(0, PEER_SUB // PEER_N_KEYS, PEER_PAIR):
                for lc in range(tt // LANES):
                    lanes = slice(lc * LANES, (lc + 1) * LANES)
                    i1s = [r0 // PEER_N_KEYS + rp + k for k in range(PEER_PAIR)]
                    ws = [jnp.zeros((PEER_N_KEYS, LANES), F32) for _ in i1s]
                    for h in range(n_heads):
                        s2 = s2_ref[h, :, lanes]
                        a2 = a2_ref[h, :, lanes]
                        for k, i1 in enumerate(i1s):
                            picked = jnp.where(s2 >= tau_ref[h, i1:i1 + 1, lanes], a2, 0.0)
                            ws[k] = ws[k] + a1_ref[h, i1:i1 + 1, lanes] * picked
                    for k, i1 in enumerate(i1s):
                        rows = slice((rp + k) * PEER_N_KEYS, (rp + k + 1) * PEER_N_KEYS)
                        x = hid[rows, lanes]
                        th = jnp.tanh(x * (x * x * (GELU_C * 0.044715) + GELU_C))
                        g = (x * th + x) * ws[k]
                        g_ref[slot, r0 + rows.start:r0 + rows.stop, lanes] = g.astype(g_ref.dtype)

    @pl.when(c < n_chunks)
    def _():
        acc_ref[...] += second_matmul()
        first_matmul_and_gate()

    @pl.when(c == n_chunks)
    def _():
        y = (acc_ref[...] + second_matmul()).T
        o_ref[...] = _layer_norm(DEEPNORM_ALPHA * x_ref[...] + y, gain_ref[...], bias_ref[...])


def _peer(xt, u_bf16, vt_bf16, tau, a1, s2, a2, x, gain, bias):
    d, t = xt.shape
    n_exp = u_bf16.shape[0]
    tt = min(PEER_TT, t)
    ec = PEER_EC
    n_chunks = n_exp // ec
    rows = ec // PEER_N_KEYS
    last = n_chunks - 1
    cur = lambda c: jnp.minimum(c, last)
    prev = lambda c: jnp.maximum(c - 1, 0)
    chunk = pl.BlockSpec((PEER_HEADS, rows, tt), lambda i, c: (0, cur(c), i))
    full = pl.BlockSpec((PEER_HEADS, PEER_N_KEYS, tt), lambda i, c: (0, 0, i))
    kern = functools.partial(_peer_kernel, n_chunks=n_chunks)
    return pl.pallas_call(
        kern,
        grid=(t // tt, n_chunks + 1),
        in_specs=[pl.BlockSpec((d, tt), lambda i, c: (0, i)),
                  pl.BlockSpec((ec, d), lambda i, c: (cur(c), 0)),
                  pl.BlockSpec((d, ec), lambda i, c: (0, prev(c))),
                  chunk, chunk, full, full,
                  pl.BlockSpec((tt, d), lambda i, c: (i, 0), pipeline_mode=pl.Buffered(1)),
                  pl.BlockSpec((1, d), lambda i, c: (0, 0)),
                  pl.BlockSpec((1, d), lambda i, c: (0, 0))],
        out_specs=pl.BlockSpec((tt, d), lambda i, c: (i, 0)),
        out_shape=jax.ShapeDtypeStruct((t, d), F32),
        scratch_shapes=[pltpu.VMEM((d, tt), F32), pltpu.VMEM((2, ec, tt), BF16)],
        compiler_params=_params("parallel", "arbitrary"),
        name="peer_dense",
    )(xt, u_bf16, vt_bf16, tau, a1, s2, a2, x, gain, bias)


def _rope_tables(positions):
    pos = positions.reshape(-1).astype(F32)[:, None]

    def table(dim):
        inv_freq = ROPE_THETA ** (-jnp.arange(0, dim, 2, dtype=F32) / dim)
        ang = pos * inv_freq
        return jnp.cos(ang), jnp.sin(ang)

    ca, sa = table(MOBA_HEAD_DIM)
    cb, sb = table(DIFF_QK_DIM)
    return (jnp.concatenate([ca, ca], axis=1), jnp.concatenate([-sa, sa], axis=1),
            jnp.concatenate([cb, cb, cb, cb], axis=1), jnp.concatenate([-sb, sb, -sb, sb], axis=1))


def kernel(x, positions, w_in, w_branch_a, w_branch_b, w_out, diff_lambda, diff_norm_gain, ln1_gain, ln1_bias,
           ln2_gain, ln2_bias, peer_w_query, peer_subkeys, peer_u, peer_v):
    batch, seq, d = x.shape
    xf = x.reshape(batch * seq, d)
    tables = _rope_tables(positions)
    for layer in range(DEPTH):
        lambda_init = 0.8 - 0.6 * math.exp(-0.3 * layer)
        proj = _in_proj(xf, w_in[layer].astype(BF16), tables)
        out_a = _moba(proj, batch, seq)
        out_b = _diff(proj, diff_lambda[layer], diff_norm_gain[layer], batch, seq, lambda_init)
        merged = _merge(out_a, out_b, proj, w_branch_a[layer].astype(BF16), w_branch_b[layer].astype(BF16))
        x1, x1t = _out_ln(xf, merged, w_out[layer].astype(BF16), ln1_gain[layer][None], ln1_bias[layer][None])
        tau, a1, s2, a2 = _route(x1t, peer_w_query[layer].T.astype(BF16), peer_subkeys[layer].astype(BF16))
        xf = _peer(x1t, peer_u[layer].astype(BF16), peer_v[layer].T.astype(BF16), tau, a1, s2, a2,
                   x1, ln2_gain[layer][None], ln2_bias[layer][None])
    return xf.reshape(batch, seq, d)
```
